```python
import functools
import jax, jax.numpy as jnp
from jax import lax
import numpy as np

D_MODEL = 1024
BATCH = 16
SEQ = 256
DEPTH = 2
DEC_BATCH = 4
DEC_SEQ = 2048
PAST_LEN = 256

GRID_W = 64
HEAD_DIM = 64
WIDTH_A = D_MODEL // 2
N_HEADS_A = WIDTH_A // HEAD_DIM
WIN_R = 8
WIN_C = 16
WIDTH_B = D_MODEL // 4
N_GROUPS_B = 4
GROUP_B = WIDTH_B // N_GROUPS_B
WIDTH_C = D_MODEL // 4
POOL_WINDOWS = (2, 4, 8, 16)
GROUP_C = WIDTH_C // len(POOL_WINDOWS)
Q_BLOCK = 128
EPS = 1e-6
NEG_INF = -1e30
BASE_W = 4 * WIDTH_A + 2 * WIDTH_B + 2 * WIDTH_C
IN_WIDTH = BASE_W + 3 * D_MODEL
SPLIT_POINTS = (WIDTH_A, 2 * WIDTH_A, 3 * WIDTH_A, 4 * WIDTH_A,
                4 * WIDTH_A + WIDTH_B, 4 * WIDTH_A + 2 * WIDTH_B,
                4 * WIDTH_A + 2 * WIDTH_B + WIDTH_C, BASE_W,
                BASE_W + D_MODEL, BASE_W + 2 * D_MODEL)

kernel_name = 'hybrid_natten_fnet_pool_diffusion_step'


def _rmsnorm(x, g):
    xf = x.astype(jnp.float32)
    y = xf * lax.rsqrt(jnp.mean(xf * xf, axis=-1, keepdims=True) + EPS)
    return y.astype(x.dtype) * g


def _ada(cond, w_ada, b_ada):
    a = jax.nn.silu(cond) @ w_ada + b_ada
    return jnp.split(a, 3, axis=-1)


def _heads(t):
    b, l, _ = t.shape
    return t.reshape(b, l, N_HEADS_A, HEAD_DIM).transpose(0, 2, 1, 3)


def _ctx_attention(q, k, v):
    b, h, l, d = q.shape
    nb = l // Q_BLOCK
    qb = q.reshape(b, h, nb, Q_BLOCK, d).transpose(2, 0, 1, 3, 4)
    scale = HEAD_DIM ** -0.5

    def block(qi):
        s = jnp.einsum('bhqd,bhkd->bhqk', qi, k).astype(jnp.float32) * scale
        p = jax.nn.softmax(s, axis=-1).astype(v.dtype)
        return jnp.einsum('bhqk,bhkd->bhqd', p, v)

    o = lax.map(block, qb)
    return o.transpose(1, 0, 3, 2, 4).reshape(b, l, h * d)


def _na_attention(q, k, v, k_ctx, v_ctx, rpb):
    b, h, n, d = q.shape
    rows = n // GRID_W
    wr = min(WIN_R, rows)
    nw = wr * GRID_W
    kg = k.reshape(b, h, rows, GRID_W, d)
    vg = v.reshape(b, h, rows, GRID_W, d)
    q_rows = q.reshape(b, h, rows, GRID_W, d).transpose(2, 0, 1, 3, 4)
    row_start = jnp.clip(jnp.arange(rows) - wr // 2, 0, rows - wr)
    col = jnp.arange(GRID_W)
    col_start = jnp.clip(col - WIN_C // 2, 0, GRID_W - WIN_C)
    col_ok = (col[None, :] >= col_start[:, None]) & (col[None, :] < col_start[:, None] + WIN_C)
    win_mask = jnp.broadcast_to(col_ok[:, None, :], (GRID_W, wr, GRID_W)).reshape(GRID_W, nw)
    col_idx = jnp.clip(col[None, :] - col[:, None] + WIN_C - 1, 0, 2 * WIN_C - 2)
    scale = HEAD_DIM ** -0.5

    def row_block(args):
        qr, r = args
        rs = row_start[r]
        kw = lax.dynamic_slice_in_dim(kg, rs, wr, axis=2).reshape(b, h, nw, d)
        vw = lax.dynamic_slice_in_dim(vg, rs, wr, axis=2).reshape(b, h, nw, d)
        row_idx = rs + jnp.arange(wr) - r + WIN_R - 1
        bias = rpb[:, row_idx[None, :, None], col_idx[:, None, :]].reshape(h, GRID_W, nw)
        s_win = jnp.einsum('bhqd,bhkd->bhqk', qr, kw).astype(jnp.float32) * scale + bias.astype(jnp.float32)
        s_win = jnp.where(win_mask, s_win, NEG_INF)
        s_ctx = jnp.einsum('bhqd,bhcd->bhqc', qr, k_ctx).astype(jnp.float32) * scale
        p = jax.nn.softmax(jnp.concatenate([s_win, s_ctx], axis=-1), axis=-1).astype(v.dtype)
        return (jnp.einsum('bhqk,bhkd->bhqd', p[..., :nw], vw)
                + jnp.einsum('bhqc,bhcd->bhqd', p[..., nw:], v_ctx))

    o = lax.map(row_block, (q_rows, jnp.arange(rows)))
    return o.transpose(1, 0, 3, 2, 4).reshape(b, n, h * d)


def _fourier(u, w_fnet):
    b, l, _ = u.shape
    ug = u.reshape(b, l, N_GROUPS_B, GROUP_B).astype(jnp.float32)
    f = jnp.fft.fft2(ug, axes=(1, 3), norm='ortho').real
    return f.reshape(b, l, WIDTH_B).astype(u.dtype) @ w_fnet


def _pool(u, w_pool, pool_scale):
    b, l, _ = u.shape
    uf = u.astype(jnp.float32)
    csum = jnp.concatenate([jnp.zeros((b, 1, WIDTH_C), jnp.float32), jnp.cumsum(uf, axis=1)], axis=1)
    t = jnp.arange(l)
    outs = []
    for gi, w in enumerate(POOL_WINDOWS):
        lo = jnp.clip(t - w // 2, 0, l)
        hi = jnp.clip(t + w // 2, 0, l)
        cg = csum[:, :, gi * GROUP_C:(gi + 1) * GROUP_C]
        mean = (cg[:, hi] - cg[:, lo]) / (hi - lo).astype(jnp.float32)[None, :, None]
        outs.append(mean - uf[:, :, gi * GROUP_C:(gi + 1) * GROUP_C])
    dlt = jnp.stack(outs, axis=2).astype(u.dtype)
    y = jnp.einsum('blgc,gce->blge', dlt, w_pool).reshape(b, l, WIDTH_C)
    return y * pool_scale


def _layer(x, shift, scale, gate, lp, attend):
    (norm_g, w_in, q_g, k_g, w_fnet, w_pool, pool_scale, p_a, p_b, p_c, w_o) = lp
    h = _rmsnorm(x, norm_g) * (1 + scale) + shift
    q, k, v, z_a, u_b, z_b, u_c, z_c, g_a, g_b, g_c = jnp.split(h @ w_in, SPLIT_POINTS, axis=-1)
    q = _rmsnorm(_heads(q), q_g)
    k = _rmsnorm(_heads(k), k_g)
    v = _heads(v)
    y_a = attend(q, k, v)
    y_b = _fourier(u_b, w_fnet)
    y_c = _pool(u_c, w_pool, pool_scale)
    merged = (jax.nn.sigmoid(g_a) * ((jax.nn.silu(z_a) * y_a) @ p_a)
              + jax.nn.sigmoid(g_b) * ((jax.nn.silu(z_b) * y_b) @ p_b)
              + jax.nn.sigmoid(g_c) * ((jax.nn.silu(z_c) * y_c) @ p_c))
    return x + gate * (merged @ w_o), k, v


def setup_inputs(seed: int = 0) -> dict:
    key = jax.random.key(seed)
    ks = jax.random.split(key, 20)

    def nrm(k, shape, s):
        return jax.random.normal(k, shape, jnp.float32) * s

    return {
        'x_prompt': nrm(ks[0], (BATCH, SEQ, D_MODEL), 1.0),
        'x_sample': nrm(ks[1], (DEC_BATCH, DEC_SEQ, D_MODEL), 1.0),
        'cache_k': nrm(ks[2], (DEC_BATCH, DEPTH, N_HEADS_A, PAST_LEN, HEAD_DIM), 1.0),
        'cache_v': nrm(ks[3], (DEC_BATCH, DEPTH, N_HEADS_A, PAST_LEN, HEAD_DIM), 1.0),
        'c': nrm(ks[4], (DEC_BATCH, D_MODEL), 1.0),
        'c_ctx': nrm(ks[5], (D_MODEL,), 1.0),
        'norm_g': 1.0 + nrm(ks[6], (DEPTH, D_MODEL), 0.02),
        'w_ada': nrm(ks[7], (DEPTH, D_MODEL, 3 * D_MODEL), 0.5 * D_MODEL ** -0.5),
        'b_ada': nrm(ks[8], (DEPTH, 3 * D_MODEL), 0.01),
        'w_in': nrm(ks[9], (DEPTH, D_MODEL, IN_WIDTH), D_MODEL ** -0.5),
        'q_norm_g': 1.0 + nrm(ks[10], (DEPTH, HEAD_DIM), 0.02),
        'k_norm_g': 1.0 + nrm(ks[11], (DEPTH, HEAD_DIM), 0.02),
        'rpb': nrm(ks[12], (DEPTH, N_HEADS_A, 2 * WIN_R - 1, 2 * WIN_C - 1), 0.1),
        'w_fnet': nrm(ks[13], (DEPTH, WIDTH_B, WIDTH_B), WIDTH_B ** -0.5),
        'w_pool': nrm(ks[14], (DEPTH, len(POOL_WINDOWS), GROUP_C, GROUP_C), GROUP_C ** -0.5),
        'pool_scale': 1.0 + nrm(ks[15], (DEPTH, WIDTH_C), 0.02),
        'p_a': nrm(ks[16], (DEPTH, WIDTH_A, D_MODEL), WIDTH_A ** -0.5),
        'p_b': nrm(ks[17], (DEPTH, WIDTH_B, D_MODEL), WIDTH_B ** -0.5),
        'p_c': nrm(ks[18], (DEPTH, WIDTH_C, D_MODEL), WIDTH_C ** -0.5),
        'w_o': nrm(ks[19], (DEPTH, D_MODEL, D_MODEL), D_MODEL ** -0.5),
    }


def reference(x_prompt, x_sample, cache_k, cache_v, c, c_ctx, norm_g, w_ada, b_ada, w_in,
              q_norm_g, k_norm_g, rpb, w_fnet, w_pool, pool_scale, p_a, p_b, p_c, w_o):
    xp = x_prompt
    xs = x_sample
    new_k = []
    new_v = []
    for l in range(DEPTH):
        lp = (norm_g[l], w_in[l], q_norm_g[l], k_norm_g[l], w_fnet[l], w_pool[l],
              pool_scale[l], p_a[l], p_b[l], p_c[l], w_o[l])
        shift, scale, gate = _ada(c_ctx, w_ada[l], b_ada[l])
        xp, k_l, v_l = _layer(xp, shift, scale, gate, lp, _ctx_attention)
        new_k.append(k_l)
        new_v.append(v_l)
        shift, scale, gate = _ada(c[:, None, :], w_ada[l], b_ada[l])
        attend = functools.partial(_na_attention, k_ctx=cache_k[:, l], v_ctx=cache_v[:, l], rpb=rpb[l])
        xs, _, _ = _layer(xs, shift, scale, gate, lp, attend)
    return (xp, xs, jnp.stack(new_k, axis=1), jnp.stack(new_v, axis=1))
```

```python
import functools

import numpy as np
import jax
import jax.numpy as jnp
from jax.experimental import pallas as pl
from jax.experimental.pallas import tpu as pltpu

D_MODEL = 1024
DEPTH = 2
GRID_W = 64
HEAD_DIM = 64
WIDTH_A = 512
N_HEADS = 8
WIN_R = 8
WIN_C = 16
WIDTH_B = 256
N_GROUPS_B = 4
GROUP_B = 64
WIDTH_C = 256
POOL_WINDOWS = (2, 4, 8, 16)
GROUP_C = 64
EPS = 1e-6
NEG_INF = -1e30
BASE_W = 4 * WIDTH_A + 2 * WIDTH_B + 2 * WIDTH_C
IN_WIDTH = BASE_W + 3 * D_MODEL

OFF_Q, OFF_K, OFF_V, OFF_ZA = 0, 512, 1024, 1536
OFF_UB, OFF_ZB, OFF_UC, OFF_ZC = 2048, 2304, 2560, 2816
OFF_GA, OFF_GB, OFF_GC = 3072, 4096, 5120

POOL_HALO = 16
POOL_TILE = 128
VMEM_LIMIT = 56 * 1024 * 1024

BF16 = jnp.bfloat16
F32 = jnp.float32


def _params(n_axes, vmem=VMEM_LIMIT):
    return pltpu.CompilerParams(dimension_semantics=("arbitrary",) * n_axes, vmem_limit_bytes=vmem)


def _const_spec(shape):
    zeros = (0,) * len(shape)
    return pl.BlockSpec(shape, lambda *_: zeros)


def _sigmoid(x):
    return 1.0 / (1.0 + jnp.exp(-x))


def _ada_kernel(cond_ref, w_ref, b_ref, o_ref):
    cnd = cond_ref[...]
    a = cnd * _sigmoid(cnd)
    o_ref[0] = jnp.dot(a, w_ref[0], preferred_element_type=F32) + b_ref[0]


def _ada(cond, w_ada, b_ada):
    tn = 1024
    return pl.pallas_call(
        _ada_kernel,
        grid=(DEPTH, 3 * D_MODEL // tn),
        in_specs=[
            pl.BlockSpec((8, D_MODEL), lambda l, j: (0, 0)),
            pl.BlockSpec((1, D_MODEL, tn), lambda l, j: (l, 0, j)),
            pl.BlockSpec((1, 1, tn), lambda l, j: (l, 0, j)),
        ],
        out_specs=pl.BlockSpec((1, 8, tn), lambda l, j: (l, 0, j)),
        out_shape=jax.ShapeDtypeStruct((DEPTH, 8, 3 * D_MODEL), F32),
        compiler_params=_params(2),
        name="ada",
    )(cond, w_ada, b_ada.reshape(DEPTH, 1, 3 * D_MODEL))


def _inproj_kernel(x_ref, mod_ref, ng_ref, w_ref, qg_ref, kg_ref, hm_ref, bd_ref, *outs, kv_f32):
    if kv_f32:
        (q_o, k_o, v_o, za_o, ucs_o, zb_o, uc_o, zc_o, ga_o, gb_o, gc_o, kf_o, vf_o) = outs
    else:
        (q_o, k_o, v_o, za_o, ucs_o, zb_o, uc_o, zc_o, ga_o, gb_o, gc_o) = outs
    x = x_ref[...]
    shift = mod_ref[0, 0:1, :]
    scale = mod_ref[0, 1:2, :]
    y = x * jax.lax.rsqrt(jnp.mean(x * x, axis=-1, keepdims=True) + EPS)
    h = (y * ng_ref[...]) * (1.0 + scale) + shift
    hb = h.astype(BF16)

    def proj(off, width):
        return jnp.dot(hb, w_ref[:, off:off + width], preferred_element_type=F32)

    def head_norm(t, g_ref):
        ms = jnp.dot((t * t).astype(BF16), hm_ref[...], preferred_element_type=F32)
        return t * jax.lax.rsqrt(ms + EPS) * g_ref[...]

    q = head_norm(proj(OFF_Q, WIDTH_A), qg_ref)
    q_o[...] = (q * (HEAD_DIM ** -0.5)).astype(BF16)
    k = head_norm(proj(OFF_K, WIDTH_A), kg_ref)
    k_o[...] = k.astype(BF16)
    v = proj(OFF_V, WIDTH_A)
    v_o[...] = v.astype(BF16)
    if kv_f32:
        kf_o[...] = k
        vf_o[...] = v
    za = proj(OFF_ZA, WIDTH_A)
    za_o[...] = (za * _sigmoid(za)).astype(BF16)
    ub = proj(OFF_UB, WIDTH_B)
    ucs_o[...] = jnp.dot(ub.astype(BF16), bd_ref[...], preferred_element_type=F32).astype(BF16)
    zb = proj(OFF_ZB, WIDTH_B)
    zb_o[...] = (zb * _sigmoid(zb)).astype(BF16)
    uc_o[...] = proj(OFF_UC, WIDTH_C)
    zc = proj(OFF_ZC, WIDTH_C)
    zc_o[...] = (zc * _sigmoid(zc)).astype(BF16)
    ga_o[...] = _sigmoid(proj(OFF_GA, D_MODEL)).astype(BF16)
    gb_o[...] = _sigmoid(proj(OFF_GB, D_MODEL)).astype(BF16)
    gc_o[...] = _sigmoid(proj(OFF_GC, D_MODEL)).astype(BF16)


def _inproj(x2, mod, seq, ng, w_in, qg, kg, hm, bd, kv_f32, tm=256):
    t = x2.shape[0]
    tiles_per_seq = seq // tm
    row = lambda w: pl.BlockSpec((tm, w), lambda i: (i, 0))
    widths = [(WIDTH_A, BF16)] * 4 + [(2 * WIDTH_B, BF16), (WIDTH_B, BF16), (WIDTH_C, F32), (WIDTH_C, BF16)] \
        + [(D_MODEL, BF16)] * 3
    if kv_f32:
        widths = widths + [(WIDTH_A, F32)] * 2
    return pl.pallas_call(
        functools.partial(_inproj_kernel, kv_f32=kv_f32),
        grid=(t // tm,),
        in_specs=[
            row(D_MODEL),
            pl.BlockSpec((1, 3, D_MODEL), lambda i: (i // tiles_per_seq, 0, 0)),
            _const_spec((1, D_MODEL)),
            _const_spec((D_MODEL, IN_WIDTH)),
            _const_spec((1, WIDTH_A)),
            _const_spec((1, WIDTH_A)),
            _const_spec((WIDTH_A, WIDTH_A)),
            _const_spec((WIDTH_B, 2 * WIDTH_B)),
        ],
        out_specs=[row(w) for w, _ in widths],
        out_shape=[jax.ShapeDtypeStruct((t, w), dt) for w, dt in widths],
        compiler_params=_params(1),
        name="inproj",
    )(x2, mod, ng, w_in, qg, kg, hm, bd)


def _ctx_attn_kernel(q_ref, k_ref, v_ref, za_ref, o_ref):
    outs = []
    for hh in range(N_HEADS):
        sl = slice(hh * HEAD_DIM, (hh + 1) * HEAD_DIM)
        qh = q_ref[:, sl]
        kh = k_ref[:, sl]
        vh = v_ref[:, sl]
        s = jax.lax.dot_general(qh, kh, (((1,), (1,)), ((), ())), preferred_element_type=F32)
        m = jnp.max(s, axis=-1, keepdims=True)
        p = jnp.exp(s - m)
        l = jnp.sum(p, axis=-1, keepdims=True)
        o = jnp.dot(p.astype(BF16), vh, preferred_element_type=F32)
        outs.append(o / l)
    ya = jnp.concatenate(outs, axis=-1)
    o_ref[...] = (ya * za_ref[...].astype(F32)).astype(BF16)


def _ctx_attn(q, k, v, za, seq):
    t = q.shape[0]
    blk = pl.BlockSpec((seq, WIDTH_A), lambda b: (b, 0))
    return pl.pallas_call(
        _ctx_attn_kernel,
        grid=(t // seq,),
        in_specs=[blk, blk, blk, blk],
        out_specs=blk,
        out_shape=jax.ShapeDtypeStruct((t, WIDTH_A), BF16),
        compiler_params=_params(1),
        name="ctx_attn",
    )(q, k, v, za)


def _bias_kernel(src_ref, o_ref):
    dd = pl.program_id(1)
    qc = jax.lax.broadcasted_iota(jnp.int32, (GRID_W, 128), 0)
    lane = jax.lax.broadcasted_iota(jnp.int32, (GRID_W, 128), 1)
    kc = jnp.where(lane < GRID_W, lane, lane - GRID_W)
    cs = jnp.clip(qc - WIN_C // 2, 0, GRID_W - WIN_C)
    valid = (kc >= cs) & (kc < cs + WIN_C)
    for hh in range(N_HEADS):
        for mm in range(WIN_R // 2):
            row = src_ref[0, hh, pl.ds(dd + 2 * mm, 1), :]
            x = jnp.broadcast_to(row, (GRID_W, 128))
            t = pltpu.roll(x, 0, 1, stride=1, stride_axis=0)
            o_ref[0, 0, hh, :, mm * 128:(mm + 1) * 128] = jnp.where(valid, t, NEG_INF)


def _bias_table(rpb):
    src = jnp.zeros((DEPTH, N_HEADS, 16, 128), F32)
    nj = 2 * WIN_R - 2
    src = src.at[:, :, :nj, 0:WIN_C].set(rpb[:, :, :nj, WIN_C - 1:])
    src = src.at[:, :, :nj, 128 - (WIN_C - 1):].set(rpb[:, :, :nj, :WIN_C - 1])
    src = src.at[:, :, :nj, GRID_W - (WIN_C - 1):GRID_W + WIN_C].set(rpb[:, :, 1:nj + 1, :])
    return pl.pallas_call(
        _bias_kernel,
        grid=(DEPTH, WIN_R),
        in_specs=[pl.BlockSpec((1, N_HEADS, 16, 128), lambda l, d: (l, 0, 0, 0))],
        out_specs=pl.BlockSpec((1, 1, N_HEADS, GRID_W, WIN_R * GRID_W), lambda l, d: (l, d, 0, 0, 0)),
        out_shape=jax.ShapeDtypeStruct((DEPTH, WIN_R, N_HEADS, GRID_W, WIN_R * GRID_W), F32),
        compiler_params=_params(2),
        name="rpb_table",
    )(src)


def _row_start(r, rows):
    return jnp.clip(r - WIN_R // 2, 0, rows - WIN_R)


def _na_attn_kernel(q_ref, k_ref, v_ref, kc_ref, vc_ref, bias_ref, za_ref, o_ref, *, rows):
    r = pl.program_id(1)
    rs = _row_start(r, rows)
    start = pl.multiple_of(rs * GRID_W, GRID_W)
    nw = WIN_R * GRID_W
    outs = []
    for hh in range(N_HEADS):
        sl = slice(hh * HEAD_DIM, (hh + 1) * HEAD_DIM)
        qh = q_ref[:, sl]
        kw = k_ref[pl.ds(start, nw), sl]
        vw = v_ref[pl.ds(start, nw), sl]
        kc = kc_ref[0, 0, hh].astype(BF16)
        vc = vc_ref[0, 0, hh].astype(BF16)
        s_win = jax.lax.dot_general(qh, kw, (((1,), (1,)), ((), ())), preferred_element_type=F32)
        s_win = s_win + bias_ref[0, hh]
        s_ctx = jax.lax.dot_general(qh, kc, (((1,), (1,)), ((), ())), preferred_element_type=F32)
        m = jnp.maximum(jnp.max(s_win, axis=-1, keepdims=True), jnp.max(s_ctx, axis=-1, keepdims=True))
        p_win = jnp.exp(s_win - m)
        p_ctx = jnp.exp(s_ctx - m)
        l = jnp.sum(p_win, axis=-1, keepdims=True) + jnp.sum(p_ctx, axis=-1, keepdims=True)
        o = (jnp.dot(p_win.astype(BF16), vw, preferred_element_type=F32)
             + jnp.dot(p_ctx.astype(BF16), vc, preferred_element_type=F32))
        outs.append(o / l)
    ya = jnp.concatenate(outs, axis=-1)
    o_ref[...] = (ya * za_ref[...].astype(F32)).astype(BF16)


def _na_attn(q, k, v, cache_k, cache_v, layer, bias, za, seq):
    t = q.shape[0]
    nb = t // seq
    rows = seq // GRID_W
    wr = min(WIN_R, rows)
    assert wr == WIN_R
    past = cache_k.shape[3]
    qblk = pl.BlockSpec((GRID_W, WIDTH_A), lambda b, r: (b * rows + r, 0))
    kvblk = pl.BlockSpec((seq, WIDTH_A), lambda b, r: (b, 0))
    cblk = pl.BlockSpec((1, 1, N_HEADS, past, HEAD_DIM), lambda b, r: (b, layer, 0, 0, 0))
    bblk = pl.BlockSpec((1, N_HEADS, GRID_W, WIN_R * GRID_W),
                        lambda b, r: (_row_start(r, rows) - r + WIN_R - 1, 0, 0, 0))
    return pl.pallas_call(
        functools.partial(_na_attn_kernel, rows=rows),
        grid=(nb, rows),
        in_specs=[qblk, kvblk, kvblk, cblk, cblk, bblk, qblk],
        out_specs=qblk,
        out_shape=jax.ShapeDtypeStruct((t, WIDTH_A), BF16),
        compiler_params=_params(2),
        name="na_attn",
    )(q, k, v, cache_k, cache_v, bias, za)


def _fourier_kernel(c_ref, s_ref, ucs_ref, wf_ref, zb_ref, o_ref):
    uc = ucs_ref[:, 0:WIDTH_B]
    us = ucs_ref[:, WIDTH_B:2 * WIDTH_B]
    y = (jnp.dot(c_ref[...], uc, preferred_element_type=F32)
         - jnp.dot(s_ref[...], us, preferred_element_type=F32))
    y2 = jnp.dot(y.astype(BF16), wf_ref[...], preferred_element_type=F32)
    o_ref[...] = (y2 * zb_ref[...].astype(F32)).astype(BF16)


def _fourier(ucs, zb, cmat, smat, wf, seq, tk):
    t = ucs.shape[0]
    nb = t // seq
    nk = seq // tk
    return pl.pallas_call(
        _fourier_kernel,
        grid=(nk, nb),
        in_specs=[
            pl.BlockSpec((tk, seq), lambda k, b: (k, 0)),
            pl.BlockSpec((tk, seq), lambda k, b: (k, 0)),
            pl.BlockSpec((seq, 2 * WIDTH_B), lambda k, b: (b, 0)),
            _const_spec((WIDTH_B, WIDTH_B)),
            pl.BlockSpec((tk, WIDTH_B), lambda k, b: (b * nk + k, 0)),
        ],
        out_specs=pl.BlockSpec((tk, WIDTH_B), lambda k, b: (b * nk + k, 0)),
        out_shape=jax.ShapeDtypeStruct((t, WIDTH_B), BF16),
        compiler_params=_params(2),
        name="fourier",
    )(cmat, smat, ucs, wf, zb)


def _dft_consts(seq):
    n = np.arange(seq, dtype=np.int64)
    ang = 2.0 * np.pi * ((n[:, None] * n[None, :]) % seq).astype(np.float64) / seq
    nrm = 1.0 / np.sqrt(float(seq) * GROUP_B)
    cmat = (np.cos(ang) * nrm).astype(np.float32)
    smat = (np.sin(ang) * nrm).astype(np.float32)
    return cmat, smat


def _group_dft_const():
    m = np.arange(GROUP_B, dtype=np.int64)
    ang = 2.0 * np.pi * ((m[:, None] * m[None, :]) % GROUP_B).astype(np.float64) / GROUP_B
    eye = np.eye(N_GROUPS_B)
    return np.concatenate([np.kron(eye, np.cos(ang)), np.kron(eye, np.sin(ang))], axis=1).astype(np.float32)


def _pool_kernel(u_ref, zc_ref, w_ref, ps_ref, o_ref, pad_ref, *, seq):
    halo = POOL_HALO
    pad_ref[0:halo, :] = jnp.zeros((halo, WIDTH_C), F32)
    pad_ref[halo + seq:2 * halo + seq, :] = jnp.zeros((halo, WIDTH_C), F32)
    pad_ref[halo:halo + seq, :] = u_ref[...]
    ext = POOL_TILE + 2 * halo
    lane = jax.lax.broadcasted_iota(jnp.int32, (POOL_TILE, WIDTH_C), 1)
    g0, g1, g2 = lane < GROUP_C, lane < 2 * GROUP_C, lane < 3 * GROUP_C
    half = jnp.where(g0, POOL_WINDOWS[0] // 2,
                     jnp.where(g1, POOL_WINDOWS[1] // 2,
                               jnp.where(g2, POOL_WINDOWS[2] // 2, POOL_WINDOWS[3] // 2)))
    rowi = jax.lax.broadcasted_iota(jnp.int32, (POOL_TILE, WIDTH_C), 0)
    for ti in range(seq // POOL_TILE):
        t0 = ti * POOL_TILE
        x = pad_ref[t0:t0 + ext, :]
        a1 = x + pltpu.roll(x, 1, 0)
        w4 = pltpu.roll(a1, 1, 0) + pltpu.roll(a1, ext - 1, 0)
        w8 = pltpu.roll(w4, 2, 0) + pltpu.roll(w4, ext - 2, 0)
        w16 = pltpu.roll(w8, 4, 0) + pltpu.roll(w8, ext - 4, 0)
        mid = slice(halo, halo + POOL_TILE)
        wsum = jnp.where(g0, a1[mid], jnp.where(g1, w4[mid], jnp.where(g2, w8[mid], w16[mid])))
        tpos = rowi + t0
        cnt = (jnp.minimum(tpos + half, seq) - jnp.maximum(tpos - half, 0)).astype(F32)
        dlt = wsum / cnt - x[mid]
        y = jnp.dot(dlt.astype(BF16), w_ref[...], preferred_element_type=F32) * ps_ref[...]
        o_ref[t0:t0 + POOL_TILE, :] = (y * zc_ref[t0:t0 + POOL_TILE, :].astype(F32)).astype(BF16)


def _pool(uc, zc, wp, ps, seq):
    t = uc.shape[0]
    blk = pl.BlockSpec((seq, WIDTH_C), lambda b: (b, 0))
    return pl.pallas_call(
        functools.partial(_pool_kernel, seq=seq),
        grid=(t // seq,),
        in_specs=[blk, blk, _const_spec((WIDTH_C, WIDTH_C)), _const_spec((1, WIDTH_C))],
        out_specs=blk,
        out_shape=jax.ShapeDtypeStruct((t, WIDTH_C), BF16),
        scratch_shapes=[pltpu.VMEM((seq + 2 * POOL_HALO, WIDTH_C), F32)],
        compiler_params=_params(1),
        name="pool",
    )(uc, zc, wp, ps)


def _merge_kernel(x_ref, mod_ref, ya_ref, yb_ref, yc_ref, ga_ref, gb_ref, gc_ref,
                  pa_ref, pb_ref, pc_ref, wo_ref, o_ref):
    gate = mod_ref[0, 2:3, :]
    m = (ga_ref[...].astype(F32) * jnp.dot(ya_ref[...], pa_ref[...], preferred_element_type=F32)
         + gb_ref[...].astype(F32) * jnp.dot(yb_ref[...], pb_ref[...], preferred_element_type=F32)
         + gc_ref[...].astype(F32) * jnp.dot(yc_ref[...], pc_ref[...], preferred_element_type=F32))
    o_ref[...] = x_ref[...] + gate * jnp.dot(m.astype(BF16), wo_ref[...], preferred_element_type=F32)


def _merge(x2, mod, seq, ya, yb, yc, ga, gb, gc, pa, pb, pc, wo, tm=256):
    t = x2.shape[0]
    tiles_per_seq = seq // tm
    row = lambda w: pl.BlockSpec((tm, w), lambda i: (i, 0))
    return pl.pallas_call(
        _merge_kernel,
        grid=(t // tm,),
        in_specs=[
            row(D_MODEL),
            pl.BlockSpec((1, 3, D_MODEL), lambda i: (i // tiles_per_seq, 0, 0)),
            row(WIDTH_A), row(WIDTH_B), row(WIDTH_C),
            row(D_MODEL), row(D_MODEL), row(D_MODEL),
            _const_spec((WIDTH_A, D_MODEL)), _const_spec((WIDTH_B, D_MODEL)), _const_spec((WIDTH_C, D_MODEL)),
            _const_spec((D_MODEL, D_MODEL)),
        ],
        out_specs=row(D_MODEL),
        out_shape=jax.ShapeDtypeStruct((t, D_MODEL), F32),
        compiler_params=_params(1),
        name="merge",
    )(x2, mod, ya, yb, yc, ga, gb, gc, pa, pb, pc, wo)


def _block_diag(w):
    g, c, e = w.shape
    eye = jnp.eye(g, dtype=w.dtype)
    return (eye[:, None, :, None] * w[:, :, None, :]).reshape(g * c, g * e)


def kernel(x_prompt, x_sample, cache_k, cache_v, c, c_ctx, norm_g, w_ada, b_ada, w_in, q_norm_g, k_norm_g,
           rpb, w_fnet, w_pool, pool_scale, p_a, p_b, p_c, w_o):
    nb_p, seq_p, _ = x_prompt.shape
    nb_s, seq_s, _ = x_sample.shape

    cond = jnp.concatenate([c_ctx[None, :], c, jnp.zeros((8 - 1 - nb_s, D_MODEL), F32)], axis=0)
    ada = _ada(cond, w_ada, b_ada).reshape(DEPTH, 8, 3, D_MODEL)
    bias = _bias_table(rpb)

    head_mean = jnp.asarray(np.kron(np.eye(N_HEADS), np.full((HEAD_DIM, HEAD_DIM), 1.0 / HEAD_DIM)), BF16)
    bd = jnp.asarray(_group_dft_const()).astype(BF16)
    dft_p = [jnp.asarray(m).astype(BF16) for m in _dft_consts(seq_p)]
    dft_s = [jnp.asarray(m).astype(BF16) for m in _dft_consts(seq_s)]

    xp = x_prompt.reshape(nb_p * seq_p, D_MODEL)
    xs = x_sample.reshape(nb_s * seq_s, D_MODEL)
    new_k, new_v = [], []
    for l in range(DEPTH):
        ng = norm_g[l][None, :]
        w_in_l = w_in[l].astype(BF16)
        qg = jnp.tile(q_norm_g[l], N_HEADS)[None, :]
        kg = jnp.tile(k_norm_g[l], N_HEADS)[None, :]
        wf = w_fnet[l].astype(BF16)
        wp = _block_diag(w_pool[l]).astype(BF16)
        ps = pool_scale[l][None, :]
        pa, pb, pc, wo = p_a[l].astype(BF16), p_b[l].astype(BF16), p_c[l].astype(BF16), w_o[l].astype(BF16)
        mod_p = ada[l, 0:1]
        mod_s = ada[l, 1:1 + nb_s]

        (q, k, v, za, ucs, zb, uc, zc, ga, gb, gc, kf, vf) = _inproj(
            xp, mod_p, nb_p * seq_p, ng, w_in_l, qg, kg, head_mean, bd, kv_f32=True)
        new_k.append(kf.reshape(nb_p, seq_p, N_HEADS, HEAD_DIM).transpose(0, 2, 1, 3))
        new_v.append(vf.reshape(nb_p, seq_p, N_HEADS, HEAD_DIM).transpose(0, 2, 1, 3))
        ya = _ctx_attn(q, k, v, za, seq_p)
        yb = _fourier(ucs, zb, dft_p[0], dft_p[1], wf, seq_p, tk=seq_p)
        yc = _pool(uc, zc, wp, ps, seq_p)
        xp = _merge(xp, mod_p, nb_p * seq_p, ya, yb, yc, ga, gb, gc, pa, pb, pc, wo)

        (q, k, v, za, ucs, zb, uc, zc, ga, gb, gc) = _inproj(
            xs, mod_s, seq_s, ng, w_in_l, qg, kg, head_mean, bd, kv_f32=False)
        ya = _na_attn(q, k, v, cache_k, cache_v, l, bias[l], za, seq_s)
        yb = _fourier(ucs, zb, dft_s[0], dft_s[1], wf, seq_s, tk=512)
        yc = _pool(uc, zc, wp, ps, seq_s)
        xs = _merge(xs, mod_s, seq_s, ya, yb, yc, ga, gb, gc, pa, pb, pc, wo)

    return (xp.reshape(nb_p, seq_p, D_MODEL), xs.reshape(nb_s, seq_s, D_MODEL),
            jnp.stack(new_k, axis=1), jnp.stack(new_v, axis=1))
```

```python
import functools

import numpy as np
import jax
import jax.numpy as jnp
from jax.experimental import pallas as pl
from jax.experimental.pallas import tpu as pltpu

D_MODEL = 1024
DEPTH = 2
GRID_W = 64
HEAD_DIM = 64
WIDTH_A = 512
N_HEADS = 8
WIN_R = 8
WIN_C = 16
WIDTH_B = 256
N_GROUPS_B = 4
GROUP_B = 64
WIDTH_C = 256
POOL_WINDOWS = (2, 4, 8, 16)
GROUP_C = 64
EPS = 1e-6
NEG_INF = -1e30
LOG2E = 1.4426950408889634
Q_SCALE = HEAD_DIM ** -0.5 * LOG2E
BASE_W = 4 * WIDTH_A + 2 * WIDTH_B + 2 * WIDTH_C
IN_WIDTH = BASE_W + 3 * D_MODEL

OFF_Q, OFF_K, OFF_V, OFF_ZA = 0, 512, 1024, 1536
OFF_UB, OFF_ZB, OFF_UC, OFF_ZC = 2048, 2304, 2560, 2816
OFF_GA, OFF_GB, OFF_GC = 3072, 4096, 5120

POOL_HALO = 16
POOL_TILE = 128
VMEM_LIMIT = 56 * 1024 * 1024

BF16 = jnp.bfloat16
F32 = jnp.float32


def _params(n_axes, vmem=VMEM_LIMIT):
    return pltpu.CompilerParams(dimension_semantics=("arbitrary",) * n_axes, vmem_limit_bytes=vmem)


def _const_spec(shape):
    zeros = (0,) * len(shape)
    return pl.BlockSpec(shape, lambda *_: zeros)


def _sigmoid(x):
    return 1.0 / (1.0 + jnp.exp(-x))


def _ada_kernel(cond_ref, w_ref, b_ref, o_ref):
    cnd = cond_ref[...]
    a = cnd * _sigmoid(cnd)
    o_ref[0] = jnp.dot(a, w_ref[0], preferred_element_type=F32) + b_ref[0]


def _ada(cond, w_ada, b_ada):
    tn = 1024
    return pl.pallas_call(
        _ada_kernel,
        grid=(DEPTH, 3 * D_MODEL // tn),
        in_specs=[
            pl.BlockSpec((8, D_MODEL), lambda l, j: (0, 0)),
            pl.BlockSpec((1, D_MODEL, tn), lambda l, j: (l, 0, j)),
            pl.BlockSpec((1, 1, tn), lambda l, j: (l, 0, j)),
        ],
        out_specs=pl.BlockSpec((1, 8, tn), lambda l, j: (l, 0, j)),
        out_shape=jax.ShapeDtypeStruct((DEPTH, 8, 3 * D_MODEL), F32),
        compiler_params=_params(2),
        name="ada",
    )(cond, w_ada, b_ada.reshape(DEPTH, 1, 3 * D_MODEL))


def _inproj_kernel(x_ref, mod_ref, ng_ref, w_ref, qg_ref, kg_ref, hm_ref, bd_ref, *outs, kv_f32):
    if kv_f32:
        (q_o, k_o, v_o, za_o, ucs_o, zb_o, uc_o, zc_o, ga_o, gb_o, gc_o, kf_o, vf_o) = outs
    else:
        (q_o, k_o, v_o, za_o, ucs_o, zb_o, uc_o, zc_o, ga_o, gb_o, gc_o) = outs
    x = x_ref[...]
    shift = mod_ref[0, 0:1, :]
    scale = mod_ref[0, 1:2, :]
    y = x * jax.lax.rsqrt(jnp.mean(x * x, axis=-1, keepdims=True) + EPS)
    h = (y * ng_ref[...]) * (1.0 + scale) + shift
    hb = h.astype(BF16)

    def proj(off, width):
        return jnp.dot(hb, w_ref[:, off:off + width], preferred_element_type=F32)

    def head_norm(t, g_ref):
        ms = jnp.dot((t * t).astype(BF16), hm_ref[...], preferred_element_type=F32)
        return t * jax.lax.rsqrt(ms + EPS) * g_ref[...]

    q = head_norm(proj(OFF_Q, WIDTH_A), qg_ref)
    q_o[...] = (q * Q_SCALE).astype(BF16)
    k = head_norm(proj(OFF_K, WIDTH_A), kg_ref)
    k_o[...] = k.astype(BF16)
    v = proj(OFF_V, WIDTH_A)
    v_o[...] = v.astype(BF16)
    if kv_f32:
        kf_o[...] = k
        vf_o[...] = v
    za = proj(OFF_ZA, WIDTH_A)
    za_o[...] = (za * _sigmoid(za)).astype(BF16)
    ub = proj(OFF_UB, WIDTH_B)
    ucs_o[...] = jnp.dot(ub.astype(BF16), bd_ref[...], preferred_element_type=F32).astype(BF16)
    zb = proj(OFF_ZB, WIDTH_B)
    zb_o[...] = (zb * _sigmoid(zb)).astype(BF16)
    uc_o[...] = proj(OFF_UC, WIDTH_C)
    zc = proj(OFF_ZC, WIDTH_C)
    zc_o[...] = (zc * _sigmoid(zc)).astype(BF16)
    ga_o[...] = _sigmoid(proj(OFF_GA, D_MODEL)).astype(BF16)
    gb_o[...] = _sigmoid(proj(OFF_GB, D_MODEL)).astype(BF16)
    gc_o[...] = _sigmoid(proj(OFF_GC, D_MODEL)).astype(BF16)


def _inproj(x2, mod, seq, ng, w_in, qg, kg, hm, bd, kv_f32, tm=256):
    t = x2.shape[0]
    tiles_per_seq = seq // tm
    row = lambda w: pl.BlockSpec((tm, w), lambda i: (i, 0))
    widths = [(WIDTH_A, BF16)] * 4 + [(2 * WIDTH_B, BF16), (WIDTH_B, BF16), (WIDTH_C, F32), (WIDTH_C, BF16)] \
        + [(D_MODEL, BF16)] * 3
    if kv_f32:
        widths = widths + [(WIDTH_A, F32)] * 2
    return pl.pallas_call(
        functools.partial(_inproj_kernel, kv_f32=kv_f32),
        grid=(t // tm,),
        in_specs=[
            row(D_MODEL),
            pl.BlockSpec((1, 3, D_MODEL), lambda i: (i // tiles_per_seq, 0, 0)),
            _const_spec((1, D_MODEL)),
            _const_spec((D_MODEL, IN_WIDTH)),
            _const_spec((1, WIDTH_A)),
            _const_spec((1, WIDTH_A)),
            _const_spec((WIDTH_A, WIDTH_A)),
            _const_spec((WIDTH_B, 2 * WIDTH_B)),
        ],
        out_specs=[row(w) for w, _ in widths],
        out_shape=[jax.ShapeDtypeStruct((t, w), dt) for w, dt in widths],
        compiler_params=_params(1),
        name="inproj",
    )(x2, mod, ng, w_in, qg, kg, hm, bd)


def _ctx_attn_kernel(q_ref, k_ref, v_ref, za_ref, o_ref):
    outs = []
    for hh in range(N_HEADS):
        sl = slice(hh * HEAD_DIM, (hh + 1) * HEAD_DIM)
        qh = q_ref[:, sl]
        kh = k_ref[:, sl]
        vh = v_ref[:, sl]
        s = jax.lax.dot_general(qh, kh, (((1,), (1,)), ((), ())), preferred_element_type=F32)
        m = jnp.max(s, axis=-1, keepdims=True)
        p = jnp.exp2(s - m)
        l = jnp.sum(p, axis=-1, keepdims=True)
        o = jnp.dot(p.astype(BF16), vh, preferred_element_type=F32)
        outs.append(o / l)
    ya = jnp.concatenate(outs, axis=-1)
    o_ref[...] = (ya * za_ref[...].astype(F32)).astype(BF16)


def _ctx_attn(q, k, v, za, seq):
    t = q.shape[0]
    blk = pl.BlockSpec((seq, WIDTH_A), lambda b: (b, 0))
    return pl.pallas_call(
        _ctx_attn_kernel,
        grid=(t // seq,),
        in_specs=[blk, blk, blk, blk],
        out_specs=blk,
        out_shape=jax.ShapeDtypeStruct((t, WIDTH_A), BF16),
        compiler_params=_params(1),
        name="ctx_attn",
    )(q, k, v, za)


NA_ROWS = 4
NA_KROWS = WIN_R + NA_ROWS - 1
NA_Q = NA_ROWS * GRID_W
NA_K = NA_KROWS * GRID_W


def _na_union_start(i, rows):
    return jnp.clip(NA_ROWS * i - WIN_R // 2, 0, rows - NA_KROWS)


def _na_cases(rows):
    cases, step_case = [], []
    for i in range(rows // NA_ROWS):
        u0 = min(max(NA_ROWS * i - WIN_R // 2, 0), rows - NA_KROWS)
        los = tuple(min(max(NA_ROWS * i + j - WIN_R // 2, 0), rows - WIN_R) - u0 for j in range(NA_ROWS))
        case = (u0 - NA_ROWS * i + WIN_R - 1, los)
        if case not in cases:
            cases.append(case)
        step_case.append(cases.index(case))
    return cases, step_case


def _select_case(idx, values):
    out = values[-1]
    for t in range(len(values) - 2, -1, -1):
        out = jnp.where(idx == t, values[t], out)
    return out


def _bias_kernel(src_ref, o_ref, *, rows):
    ty = pl.program_id(1)
    cases, _ = _na_cases(rows)
    c = _select_case(ty, [cs_[0] for cs_ in cases])
    qc = jax.lax.broadcasted_iota(jnp.int32, (GRID_W, 128), 0)
    lane = jax.lax.broadcasted_iota(jnp.int32, (GRID_W, 128), 1)
    upper = lane >= GRID_W
    kc = jnp.where(upper, lane - GRID_W, lane)
    cs = jnp.clip(qc - WIN_C // 2, 0, GRID_W - WIN_C)
    col_ok = (kc >= cs) & (kc < cs + WIN_C)
    for hh in range(N_HEADS):
        for j in range(NA_ROWS):
            lo = _select_case(ty, [cs_[1][j] for cs_ in cases])
            for p in range((NA_KROWS + 1) // 2):
                sidx = jnp.clip(2 * p - j + c + 1, 0, 15)
                row = src_ref[0, hh, pl.ds(sidx, 1), :]
                t = pltpu.roll(jnp.broadcast_to(row, (GRID_W, 128)), 0, 1, stride=1, stride_axis=0)
                kk = jnp.where(upper, 2 * p + 1, 2 * p)
                ok = col_ok & (kk >= lo) & (kk < lo + WIN_R)
                val = jnp.where(ok, t * LOG2E, NEG_INF)
                rsl = slice(j * GRID_W, (j + 1) * GRID_W)
                if 2 * p + 1 < NA_KROWS:
                    o_ref[0, 0, hh, rsl, p * 128:(p + 1) * 128] = val
                else:
                    o_ref[0, 0, hh, rsl, p * 128:p * 128 + GRID_W] = val[:, :GRID_W]


def _bias_table(rpb, rows):
    nr = 2 * WIN_R - 1
    pad = jnp.pad(rpb, ((0, 0), (0, 0), (1, 1), (0, 0)))
    lo_half, hi_half = pad[:, :, 0:nr + 1], pad[:, :, 1:nr + 2]
    src = jnp.zeros((DEPTH, N_HEADS, nr + 1, 128), F32)
    src = src.at[..., 0:WIN_C].set(lo_half[..., WIN_C - 1:])
    src = src.at[..., 128 - (WIN_C - 1):].set(lo_half[..., :WIN_C - 1])
    src = src.at[..., GRID_W - (WIN_C - 1):GRID_W + WIN_C].set(hi_half)
    ncase = len(_na_cases(rows)[0])
    return pl.pallas_call(
        functools.partial(_bias_kernel, rows=rows),
        grid=(DEPTH, ncase),
        in_specs=[pl.BlockSpec((1, N_HEADS, nr + 1, 128), lambda l, ty: (l, 0, 0, 0))],
        out_specs=pl.BlockSpec((1, 1, N_HEADS, NA_Q, NA_K), lambda l, ty: (l, ty, 0, 0, 0)),
        out_shape=jax.ShapeDtypeStruct((DEPTH, ncase, N_HEADS, NA_Q, NA_K), F32),
        compiler_params=_params(2),
        name="rpb_table",
    )(src)


def _nt_dot(a, b):
    return jax.lax.dot_general(a, b, (((1,), (1,)), ((), ())), preferred_element_type=F32)


def _na_attn_kernel(q_ref, k_ref, v_ref, kc_ref, vc_ref, bias_ref, za_ref, o_ref,
                    vaug_scr, kcp_scr, vcaug_scr, *, rows):
    i = pl.program_id(1)
    npair = N_HEADS // 2
    past = kc_ref.shape[3]

    @pl.when(i == 0)
    def _prepare():
        low = jax.lax.broadcasted_iota(jnp.int32, (v_ref.shape[0], 128), 1) < HEAD_DIM
        ones_c = jnp.ones((past, HEAD_DIM), F32)
        for g in range(npair):
            vs = v_ref[:, g * 128:(g + 1) * 128]
            vaug_scr[0, g] = jnp.where(low, vs, jnp.ones_like(vs))
            vaug_scr[1, g] = jnp.where(low, jnp.ones_like(vs), vs)
            kcp_scr[g] = jnp.concatenate([kc_ref[0, 0, 2 * g], kc_ref[0, 0, 2 * g + 1]], axis=-1).astype(BF16)
            vcaug_scr[0, g] = jnp.concatenate([vc_ref[0, 0, 2 * g], ones_c], axis=-1).astype(BF16)
            vcaug_scr[1, g] = jnp.concatenate([ones_c, vc_ref[0, 0, 2 * g + 1]], axis=-1).astype(BF16)

    start = pl.multiple_of(_na_union_start(i, rows) * GRID_W, GRID_W)
    lowq = jax.lax.broadcasted_iota(jnp.int32, (NA_Q, 128), 1) < HEAD_DIM
    for g in range(npair):
        lsl = slice(g * 128, (g + 1) * 128)
        qs = q_ref[:, lsl]
        kw = k_ref[pl.ds(start, NA_K), lsl]
        kcx = kcp_scr[g]
        res = []
        for hh in range(2):
            qz = jnp.where(lowq if hh == 0 else jnp.logical_not(lowq), qs, jnp.zeros_like(qs))
            s_w = _nt_dot(qz, kw) + bias_ref[0, 2 * g + hh]
            s_c = _nt_dot(qz, kcx)
            m = jnp.maximum(jnp.max(s_w, axis=-1, keepdims=True), jnp.max(s_c, axis=-1, keepdims=True))
            p_w = jnp.exp2(s_w - m).astype(BF16)
            p_c = jnp.exp2(s_c - m).astype(BF16)
            res.append(jnp.dot(p_w, vaug_scr[hh, g, pl.ds(start, NA_K), :], preferred_element_type=F32)
                       + jnp.dot(p_c, vcaug_scr[hh, g], preferred_element_type=F32))
        num = jnp.where(lowq, res[0], res[1])
        den = pltpu.roll(jnp.where(lowq, res[1], res[0]), HEAD_DIM, 1)
        o_ref[:, lsl] = ((num / den) * za_ref[:, lsl].astype(F32)).astype(BF16)


def _na_attn(q, k, v, cache_k, cache_v, layer, bias, za, seq):
    t = q.shape[0]
    nb = t // seq
    rows = seq // GRID_W
    assert rows % NA_ROWS == 0 and rows >= NA_KROWS
    steps = rows // NA_ROWS
    step_case = _na_cases(rows)[1]
    past = cache_k.shape[3]
    qblk = pl.BlockSpec((NA_Q, WIDTH_A), lambda b, i: (b * steps + i, 0))
    kvblk = pl.BlockSpec((seq, WIDTH_A), lambda b, i: (b, 0))
    cblk = pl.BlockSpec((1, 1, N_HEADS, past, HEAD_DIM), lambda b, i: (b, layer, 0, 0, 0))
    bblk = pl.BlockSpec((1, N_HEADS, NA_Q, NA_K), lambda b, i: (_select_case(i, step_case), 0, 0, 0))
    return pl.pallas_call(
        functools.partial(_na_attn_kernel, rows=rows),
        grid=(nb, steps),
        in_specs=[qblk, kvblk, kvblk, cblk, cblk, bblk, qblk],
        out_specs=qblk,
        out_shape=jax.ShapeDtypeStruct((t, WIDTH_A), BF16),
        scratch_shapes=[
            pltpu.VMEM((2, N_HEADS // 2, seq, 128), BF16),
            pltpu.VMEM((N_HEADS // 2, past, 128), BF16),
            pltpu.VMEM((2, N_HEADS // 2, past, 128), BF16),
        ],
        compiler_params=_params(2),
        name="na_attn",
    )(q, k, v, cache_k, cache_v, bias, za)


def _fourier_kernel(c_ref, s_ref, ucs_ref, wf_ref, zb_ref, o_ref):
    uc = ucs_ref[:, 0:WIDTH_B]
    us = ucs_ref[:, WIDTH_B:2 * WIDTH_B]
    y = (jnp.dot(c_ref[...], uc, preferred_element_type=F32)
         - jnp.dot(s_ref[...], us, preferred_element_type=F32))
    y2 = jnp.dot(y.astype(BF16), wf_ref[...], preferred_element_type=F32)
    o_ref[...] = (y2 * zb_ref[...].astype(F32)).astype(BF16)


def _fourier(ucs, zb, cmat, smat, wf, seq, tk):
    t = ucs.shape[0]
    nb = t // seq
    nk = seq // tk
    return pl.pallas_call(
        _fourier_kernel,
        grid=(nk, nb),
        in_specs=[
            pl.BlockSpec((tk, seq), lambda k, b: (k, 0)),
            pl.BlockSpec((tk, seq), lambda k, b: (k, 0)),
            pl.BlockSpec((seq, 2 * WIDTH_B), lambda k, b: (b, 0)),
            _const_spec((WIDTH_B, WIDTH_B)),
            pl.BlockSpec((tk, WIDTH_B), lambda k, b: (b * nk + k, 0)),
        ],
        out_specs=pl.BlockSpec((tk, WIDTH_B), lambda k, b: (b * nk + k, 0)),
        out_shape=jax.ShapeDtypeStruct((t, WIDTH_B), BF16),
        compiler_params=_params(2),
        name="fourier",
    )(cmat, smat, ucs, wf, zb)


def _dft_consts(seq):
    n = np.arange(seq, dtype=np.int64)
    ang = 2.0 * np.pi * ((n[:, None] * n[None, :]) % seq).astype(np.float64) / seq
    nrm = 1.0 / np.sqrt(float(seq) * GROUP_B)
    cmat = (np.cos(ang) * nrm).astype(np.float32)
    smat = (np.sin(ang) * nrm).astype(np.float32)
    return cmat, smat


def _group_dft_const():
    m = np.arange(GROUP_B, dtype=np.int64)
    ang = 2.0 * np.pi * ((m[:, None] * m[None, :]) % GROUP_B).astype(np.float64) / GROUP_B
    eye = np.eye(N_GROUPS_B)
    return np.concatenate([np.kron(eye, np.cos(ang)), np.kron(eye, np.sin(ang))], axis=1).astype(np.float32)


def _pool_kernel(u_ref, zc_ref, w_ref, ps_ref, o_ref, pad_ref, *, seq):
    halo = POOL_HALO
    pad_ref[0:halo, :] = jnp.zeros((halo, WIDTH_C), F32)
    pad_ref[halo + seq:2 * halo + seq, :] = jnp.zeros((halo, WIDTH_C), F32)
    pad_ref[halo:halo + seq, :] = u_ref[...]
    ext = POOL_TILE + 2 * halo
    lane = jax.lax.broadcasted_iota(jnp.int32, (POOL_TILE, WIDTH_C), 1)
    g0, g1, g2 = lane < GROUP_C, lane < 2 * GROUP_C, lane < 3 * GROUP_C
    half = jnp.where(g0, POOL_WINDOWS[0] // 2,
                     jnp.where(g1, POOL_WINDOWS[1] // 2,
                               jnp.where(g2, POOL_WINDOWS[2] // 2, POOL_WINDOWS[3] // 2)))
    rowi = jax.lax.broadcasted_iota(jnp.int32, (POOL_TILE, WIDTH_C), 0)
    for ti in range(seq // POOL_TILE):
        t0 = ti * POOL_TILE
        x = pad_ref[t0:t0 + ext, :]
        a1 = x + pltpu.roll(x, 1, 0)
        w4 = pltpu.roll(a1, 1, 0) + pltpu.roll(a1, ext - 1, 0)
        w8 = pltpu.roll(w4, 2, 0) + pltpu.roll(w4, ext - 2, 0)
        w16 = pltpu.roll(w8, 4, 0) + pltpu.roll(w8, ext - 4, 0)
        mid = slice(halo, halo + POOL_TILE)
        wsum = jnp.where(g0, a1[mid], jnp.where(g1, w4[mid], jnp.where(g2, w8[mid], w16[mid])))
        tpos = rowi + t0
        cnt = (jnp.minimum(tpos + half, seq) - jnp.maximum(tpos - half, 0)).astype(F32)
        dlt = wsum / cnt - x[mid]
        y = jnp.dot(dlt.astype(BF16), w_ref[...], preferred_element_type=F32) * ps_ref[...]
        o_ref[t0:t0 + POOL_TILE, :] = (y * zc_ref[t0:t0 + POOL_TILE, :].astype(F32)).astype(BF16)


def _pool(uc, zc, wp, ps, seq):
    t = uc.shape[0]
    blk = pl.BlockSpec((seq, WIDTH_C), lambda b: (b, 0))
    return pl.pallas_call(
        functools.partial(_pool_kernel, seq=seq),
        grid=(t // seq,),
        in_specs=[blk, blk, _const_spec((WIDTH_C, WIDTH_C)), _const_spec((1, WIDTH_C))],
        out_specs=blk,
        out_shape=jax.ShapeDtypeStruct((t, WIDTH_C), BF16),
        scratch_shapes=[pltpu.VMEM((seq + 2 * POOL_HALO, WIDTH_C), F32)],
        compiler_params=_params(1),
        name="pool",
    )(uc, zc, wp, ps)


def _merge_kernel(x_ref, mod_ref, ya_ref, yb_ref, yc_ref, ga_ref, gb_ref, gc_ref,
                  pa_ref, pb_ref, pc_ref, wo_ref, o_ref):
    gate = mod_ref[0, 2:3, :]
    m = (ga_ref[...].astype(F32) * jnp.dot(ya_ref[...], pa_ref[...], preferred_element_type=F32)
         + gb_ref[...].astype(F32) * jnp.dot(yb_ref[...], pb_ref[...], preferred_element_type=F32)
         + gc_ref[...].astype(F32) * jnp.dot(yc_ref[...], pc_ref[...], preferred_element_type=F32))
    o_ref[...] = x_ref[...] + gate * jnp.dot(m.astype(BF16), wo_ref[...], preferred_element_type=F32)


def _merge(x2, mod, seq, ya, yb, yc, ga, gb, gc, pa, pb, pc, wo, tm=256):
    t = x2.shape[0]
    tiles_per_seq = seq // tm
    row = lambda w: pl.BlockSpec((tm, w), lambda i: (i, 0))
    return pl.pallas_call(
        _merge_kernel,
        grid=(t // tm,),
        in_specs=[
            row(D_MODEL),
            pl.BlockSpec((1, 3, D_MODEL), lambda i: (i // tiles_per_seq, 0, 0)),
            row(WIDTH_A), row(WIDTH_B), row(WIDTH_C),
            row(D_MODEL), row(D_MODEL), row(D_MODEL),
            _const_spec((WIDTH_A, D_MODEL)), _const_spec((WIDTH_B, D_MODEL)), _const_spec((WIDTH_C, D_MODEL)),
            _const_spec((D_MODEL, D_MODEL)),
        ],
        out_specs=row(D_MODEL),
        out_shape=jax.ShapeDtypeStruct((t, D_MODEL), F32),
        compiler_params=_params(1),
        name="merge",
    )(x2, mod, ya, yb, yc, ga, gb, gc, pa, pb, pc, wo)


def _block_diag(w):
    g, c, e = w.shape
    eye = jnp.eye(g, dtype=w.dtype)
    return (eye[:, None, :, None] * w[:, :, None, :]).reshape(g * c, g * e)


def kernel(x_prompt, x_sample, cache_k, cache_v, c, c_ctx, norm_g, w_ada, b_ada, w_in, q_norm_g, k_norm_g,
           rpb, w_fnet, w_pool, pool_scale, p_a, p_b, p_c, w_o):
    nb_p, seq_p, _ = x_prompt.shape
    nb_s, seq_s, _ = x_sample.shape

    cond = jnp.concatenate([c_ctx[None, :], c, jnp.zeros((8 - 1 - nb_s, D_MODEL), F32)], axis=0)
    ada = _ada(cond, w_ada, b_ada).reshape(DEPTH, 8, 3, D_MODEL)
    bias = _bias_table(rpb, seq_s // GRID_W)

    head_mean = jnp.asarray(np.kron(np.eye(N_HEADS), np.full((HEAD_DIM, HEAD_DIM), 1.0 / HEAD_DIM)), BF16)
    bd = jnp.asarray(_group_dft_const()).astype(BF16)
    dft_p = [jnp.asarray(m).astype(BF16) for m in _dft_consts(seq_p)]
    dft_s = [jnp.asarray(m).astype(BF16) for m in _dft_consts(seq_s)]

    xp = x_prompt.reshape(nb_p * seq_p, D_MODEL)
    xs = x_sample.reshape(nb_s * seq_s, D_MODEL)
    new_k, new_v = [], []
    for l in range(DEPTH):
        ng = norm_g[l][None, :]
        w_in_l = w_in[l].astype(BF16)
        qg = jnp.tile(q_norm_g[l], N_HEADS)[None, :]
        kg = jnp.tile(k_norm_g[l], N_HEADS)[None, :]
        wf = w_fnet[l].astype(BF16)
        wp = _block_diag(w_pool[l]).astype(BF16)
        ps = pool_scale[l][None, :]
        pa, pb, pc, wo = p_a[l].astype(BF16), p_b[l].astype(BF16), p_c[l].astype(BF16), w_o[l].astype(BF16)
        mod_p = ada[l, 0:1]
        mod_s = ada[l, 1:1 + nb_s]

        (q, k, v, za, ucs, zb, uc, zc, ga, gb, gc, kf, vf) = _inproj(
            xp, mod_p, nb_p * seq_p, ng, w_in_l, qg, kg, head_mean, bd, kv_f32=True)
        new_k.append(kf.reshape(nb_p, seq_p, N_HEADS, HEAD_DIM).transpose(0, 2, 1, 3))
        new_v.append(vf.reshape(nb_p, seq_p, N_HEADS, HEAD_DIM).transpose(0, 2, 1, 3))
        ya = _ctx_attn(q, k, v, za, seq_p)
        yb = _fourier(ucs, zb, dft_p[0], dft_p[1], wf, seq_p, tk=seq_p)
        yc = _pool(uc, zc, wp, ps, seq_p)
        xp = _merge(xp, mod_p, nb_p * seq_p, ya, yb, yc, ga, gb, gc, pa, pb, pc, wo)

        (q, k, v, za, ucs, zb, uc, zc, ga, gb, gc) = _inproj(
            xs, mod_s, seq_s, ng, w_in_l, qg, kg, head_mean, bd, kv_f32=False)
        ya = _na_attn(q, k, v, cache_k, cache_v, l, bias[l], za, seq_s)
        yb = _fourier(ucs, zb, dft_s[0], dft_s[1], wf, seq_s, tk=512)
        yc = _pool(uc, zc, wp, ps, seq_s)
        xs = _merge(xs, mod_s, seq_s, ya, yb, yc, ga, gb, gc, pa, pb, pc, wo)

    return (xp.reshape(nb_p, seq_p, D_MODEL), xs.reshape(nb_s, seq_s, D_MODEL),
            jnp.stack(new_k, axis=1), jnp.stack(new_v, axis=1))
```

```python
import functools

import numpy as np
import jax
import jax.numpy as jnp
from jax.experimental import pallas as pl
from jax.experimental.pallas import tpu as pltpu

D_MODEL = 1024
DEPTH = 2
GRID_W = 64
HEAD_DIM = 64
WIDTH_A = 512
N_HEADS = 8
WIN_R = 8
WIN_C = 16
WIDTH_B = 256
N_GROUPS_B = 4
GROUP_B = 64
WIDTH_C = 256
POOL_WINDOWS = (2, 4, 8, 16)
GROUP_C = 64
EPS = 1e-6
NEG_INF = -1e30
LOG2E = 1.4426950408889634
Q_SCALE = HEAD_DIM ** -0.5 * LOG2E
BASE_W = 4 * WIDTH_A + 2 * WIDTH_B + 2 * WIDTH_C
IN_WIDTH = BASE_W + 3 * D_MODEL

OFF_Q, OFF_K, OFF_V, OFF_ZA = 0, 512, 1024, 1536
OFF_UB, OFF_ZB, OFF_UC, OFF_ZC = 2048, 2304, 2560, 2816
OFF_GA, OFF_GB, OFF_GC = 3072, 4096, 5120

POOL_HALO = 16
POOL_TILE = 128
VMEM_LIMIT = 56 * 1024 * 1024

BF16 = jnp.bfloat16
F32 = jnp.float32


def _params(n_axes, vmem=VMEM_LIMIT):
    return pltpu.CompilerParams(dimension_semantics=("arbitrary",) * n_axes, vmem_limit_bytes=vmem)


def _const_spec(shape):
    zeros = (0,) * len(shape)
    return pl.BlockSpec(shape, lambda *_: zeros)


def _sigmoid(x):
    return 1.0 / (1.0 + jnp.exp(-x))


def _ada_kernel(cond_ref, w_ref, b_ref, o_ref):
    cnd = cond_ref[...]
    a = cnd * _sigmoid(cnd)
    o_ref[0] = jnp.dot(a, w_ref[0], preferred_element_type=F32) + b_ref[0]


def _ada(cond, w_ada, b_ada):
    tn = 1024
    return pl.pallas_call(
        _ada_kernel,
        grid=(DEPTH, 3 * D_MODEL // tn),
        in_specs=[
            pl.BlockSpec((8, D_MODEL), lambda l, j: (0, 0)),
            pl.BlockSpec((1, D_MODEL, tn), lambda l, j: (l, 0, j)),
            pl.BlockSpec((1, 1, tn), lambda l, j: (l, 0, j)),
        ],
        out_specs=pl.BlockSpec((1, 8, tn), lambda l, j: (l, 0, j)),
        out_shape=jax.ShapeDtypeStruct((DEPTH, 8, 3 * D_MODEL), F32),
        compiler_params=_params(2),
        name="ada",
    )(cond, w_ada, b_ada.reshape(DEPTH, 1, 3 * D_MODEL))


def _inproj_kernel(x_ref, mod_ref, ng_ref, w_ref, qg_ref, kg_ref, hm_ref, bd_ref, *rest, layer, kv_cache):
    n_prev = 2 if (kv_cache and layer > 0) else 0
    prev, outs = rest[:n_prev], rest[n_prev:]
    if kv_cache:
        (q_o, k_o, v_o, za_o, ucs_o, zb_o, uc_o, zc_o, ga_o, gb_o, gc_o, kf_o, vf_o) = outs
    else:
        (q_o, k_o, v_o, za_o, ucs_o, zb_o, uc_o, zc_o, ga_o, gb_o, gc_o) = outs
    x = x_ref[...]
    shift = mod_ref[0, 0:1, :]
    scale = mod_ref[0, 1:2, :]
    y = x * jax.lax.rsqrt(jnp.mean(x * x, axis=-1, keepdims=True) + EPS)
    h = (y * ng_ref[...]) * (1.0 + scale) + shift
    hb = h.astype(BF16)

    def proj(off, width):
        return jnp.dot(hb, w_ref[0, :, off:off + width], preferred_element_type=F32)

    def head_norm(t, g_ref):
        ms = jnp.dot((t * t).astype(BF16), hm_ref[...], preferred_element_type=F32)
        return t * jax.lax.rsqrt(ms + EPS) * g_ref[...]

    q = head_norm(proj(OFF_Q, WIDTH_A), qg_ref)
    q_o[...] = (q * Q_SCALE).astype(BF16)
    k = head_norm(proj(OFF_K, WIDTH_A), kg_ref)
    k_o[...] = k.astype(BF16)
    v = proj(OFF_V, WIDTH_A)
    v_o[...] = v.astype(BF16)
    if kv_cache:
        if layer > 0:
            kf_o[0, 0:layer] = prev[0][0]
            vf_o[0, 0:layer] = prev[1][0]
        for hh in range(N_HEADS):
            kf_o[0, layer, hh] = k[:, hh * HEAD_DIM:(hh + 1) * HEAD_DIM]
            vf_o[0, layer, hh] = v[:, hh * HEAD_DIM:(hh + 1) * HEAD_DIM]
    za = proj(OFF_ZA, WIDTH_A)
    za_o[...] = (za * _sigmoid(za)).astype(BF16)
    ub = proj(OFF_UB, WIDTH_B)
    ucs_o[...] = jnp.dot(ub.astype(BF16), bd_ref[...], preferred_element_type=F32).astype(BF16)
    zb = proj(OFF_ZB, WIDTH_B)
    zb_o[...] = (zb * _sigmoid(zb)).astype(BF16)
    uc_o[...] = proj(OFF_UC, WIDTH_C)
    zc = proj(OFF_ZC, WIDTH_C)
    zc_o[...] = (zc * _sigmoid(zc)).astype(BF16)
    ga_o[...] = _sigmoid(proj(OFF_GA, D_MODEL)).astype(BF16)
    gb_o[...] = _sigmoid(proj(OFF_GB, D_MODEL)).astype(BF16)
    gc_o[...] = _sigmoid(proj(OFF_GC, D_MODEL)).astype(BF16)


def _layer_spec(shape, layer):
    zeros = (0,) * len(shape)
    return pl.BlockSpec((1,) + tuple(shape), lambda *_: (layer,) + zeros)


def _inproj(x2, mod, seq, mod_seq, layer, ng, w_in, qg, kg, hm, bd, prev_kv=None, kv_cache=False, tm=256):
    t = x2.shape[0]
    tiles_per_mod = mod_seq // tm
    row = lambda w: pl.BlockSpec((tm, w), lambda i: (i, 0))
    widths = [(WIDTH_A, BF16)] * 4 + [(2 * WIDTH_B, BF16), (WIDTH_B, BF16), (WIDTH_C, F32), (WIDTH_C, BF16)] \
        + [(D_MODEL, BF16)] * 3
    in_specs = [
        row(D_MODEL),
        pl.BlockSpec((1, 3, D_MODEL), lambda i: (i // tiles_per_mod, 0, 0)),
        _const_spec((1, D_MODEL)),
        _layer_spec((D_MODEL, IN_WIDTH), layer),
        _const_spec((1, WIDTH_A)),
        _const_spec((1, WIDTH_A)),
        _const_spec((WIDTH_A, WIDTH_A)),
        _const_spec((WIDTH_B, 2 * WIDTH_B)),
    ]
    args = [x2, mod, ng, w_in, qg, kg, hm, bd]
    out_specs = [row(w) for w, _ in widths]
    out_shape = [jax.ShapeDtypeStruct((t, w), dt) for w, dt in widths]
    if kv_cache:
        assert tm == seq
        cache = lambda n: pl.BlockSpec((1, n, N_HEADS, seq, HEAD_DIM), lambda i: (i, 0, 0, 0, 0))
        if layer > 0:
            in_specs += [cache(layer)] * 2
            args += list(prev_kv)
        out_specs += [cache(layer + 1)] * 2
        out_shape += [jax.ShapeDtypeStruct((t // seq, layer + 1, N_HEADS, seq, HEAD_DIM), F32)] * 2
    return pl.pallas_call(
        functools.partial(_inproj_kernel, layer=layer, kv_cache=kv_cache),
        grid=(t // tm,),
        in_specs=in_specs,
        out_specs=out_specs,
        out_shape=out_shape,
        compiler_params=_params(1),
        name="inproj",
    )(*args)


def _ctx_attn_kernel(q_ref, k_ref, v_ref, za_ref, o_ref):
    outs = []
    for hh in range(N_HEADS):
        sl = slice(hh * HEAD_DIM, (hh + 1) * HEAD_DIM)
        qh = q_ref[:, sl]
        kh = k_ref[:, sl]
        vh = v_ref[:, sl]
        s = jax.lax.dot_general(qh, kh, (((1,), (1,)), ((), ())), preferred_element_type=F32)
        m = jnp.max(s, axis=-1, keepdims=True)
        p = jnp.exp2(s - m)
        l = jnp.sum(p, axis=-1, keepdims=True)
        o = jnp.dot(p.astype(BF16), vh, preferred_element_type=F32)
        outs.append(o / l)
    ya = jnp.concatenate(outs, axis=-1)
    o_ref[...] = (ya * za_ref[...].astype(F32)).astype(BF16)


def _ctx_attn(q, k, v, za, seq):
    t = q.shape[0]
    blk = pl.BlockSpec((seq, WIDTH_A), lambda b: (b, 0))
    return pl.pallas_call(
        _ctx_attn_kernel,
        grid=(t // seq,),
        in_specs=[blk, blk, blk, blk],
        out_specs=blk,
        out_shape=jax.ShapeDtypeStruct((t, WIDTH_A), BF16),
        compiler_params=_params(1),
        name="ctx_attn",
    )(q, k, v, za)


NA_ROWS = 4
NA_KROWS = WIN_R + NA_ROWS - 1
NA_Q = NA_ROWS * GRID_W
NA_K = NA_KROWS * GRID_W


def _na_union_start(i, rows):
    return jnp.clip(NA_ROWS * i - WIN_R // 2, 0, rows - NA_KROWS)


def _na_cases(rows):
    cases, step_case = [], []
    for i in range(rows // NA_ROWS):
        u0 = min(max(NA_ROWS * i - WIN_R // 2, 0), rows - NA_KROWS)
        los = tuple(min(max(NA_ROWS * i + j - WIN_R // 2, 0), rows - WIN_R) - u0 for j in range(NA_ROWS))
        case = (u0 - NA_ROWS * i + WIN_R - 1, los)
        if case not in cases:
            cases.append(case)
        step_case.append(cases.index(case))
    return cases, step_case


def _select_case(idx, values):
    out = values[-1]
    for t in range(len(values) - 2, -1, -1):
        out = jnp.where(idx == t, values[t], out)
    return out


def _bias_kernel(src_ref, o_ref, *, rows):
    ty = pl.program_id(1)
    cases, _ = _na_cases(rows)
    c = _select_case(ty, [cs_[0] for cs_ in cases])
    qc = jax.lax.broadcasted_iota(jnp.int32, (GRID_W, 128), 0)
    lane = jax.lax.broadcasted_iota(jnp.int32, (GRID_W, 128), 1)
    upper = lane >= GRID_W
    kc = jnp.where(upper, lane - GRID_W, lane)
    cs = jnp.clip(qc - WIN_C // 2, 0, GRID_W - WIN_C)
    col_ok = (kc >= cs) & (kc < cs + WIN_C)
    for hh in range(N_HEADS):
        for j in range(NA_ROWS):
            lo = _select_case(ty, [cs_[1][j] for cs_ in cases])
            for p in range((NA_KROWS + 1) // 2):
                sidx = jnp.clip(2 * p - j + c + 1, 0, 15)
                row = src_ref[0, hh, pl.ds(sidx, 1), :]
                t = pltpu.roll(jnp.broadcast_to(row, (GRID_W, 128)), 0, 1, stride=1, stride_axis=0)
                kk = jnp.where(upper, 2 * p + 1, 2 * p)
                ok = col_ok & (kk >= lo) & (kk < lo + WIN_R)
                val = jnp.where(ok, t * LOG2E, NEG_INF)
                rsl = slice(j * GRID_W, (j + 1) * GRID_W)
                if 2 * p + 1 < NA_KROWS:
                    o_ref[0, 0, hh, rsl, p * 128:(p + 1) * 128] = val
                else:
                    o_ref[0, 0, hh, rsl, p * 128:p * 128 + GRID_W] = val[:, :GRID_W]


def _bias_table(rpb, rows):
    nr = 2 * WIN_R - 1
    pad = jnp.pad(rpb, ((0, 0), (0, 0), (1, 1), (0, 0)))
    lo_half, hi_half = pad[:, :, 0:nr + 1], pad[:, :, 1:nr + 2]
    src = jnp.zeros((DEPTH, N_HEADS, nr + 1, 128), F32)
    src = src.at[..., 0:WIN_C].set(lo_half[..., WIN_C - 1:])
    src = src.at[..., 128 - (WIN_C - 1):].set(lo_half[..., :WIN_C - 1])
    src = src.at[..., GRID_W - (WIN_C - 1):GRID_W + WIN_C].set(hi_half)
    ncase = len(_na_cases(rows)[0])
    return pl.pallas_call(
        functools.partial(_bias_kernel, rows=rows),
        grid=(DEPTH, ncase),
        in_specs=[pl.BlockSpec((1, N_HEADS, nr + 1, 128), lambda l, ty: (l, 0, 0, 0))],
        out_specs=pl.BlockSpec((1, 1, N_HEADS, NA_Q, NA_K), lambda l, ty: (l, ty, 0, 0, 0)),
        out_shape=jax.ShapeDtypeStruct((DEPTH, ncase, N_HEADS, NA_Q, NA_K), F32),
        compiler_params=_params(2),
        name="rpb_table",
    )(src)


def _nt_dot(a, b):
    return jax.lax.dot_general(a, b, (((1,), (1,)), ((), ())), preferred_element_type=F32)


def _na_attn_kernel(q_ref, k_ref, v_ref, kc_ref, vc_ref, bias_ref, za_ref, o_ref,
                    vaug_scr, kcp_scr, vcaug_scr, *, rows):
    i = pl.program_id(1)
    npair = N_HEADS // 2
    past = kc_ref.shape[3]

    @pl.when(i == 0)
    def _prepare():
        low = jax.lax.broadcasted_iota(jnp.int32, (v_ref.shape[0], 128), 1) < HEAD_DIM
        ones_c = jnp.ones((past, HEAD_DIM), F32)
        for g in range(npair):
            vs = v_ref[:, g * 128:(g + 1) * 128]
            vaug_scr[0, g] = jnp.where(low, vs, jnp.ones_like(vs))
            vaug_scr[1, g] = jnp.where(low, jnp.ones_like(vs), vs)
            kcp_scr[g] = jnp.concatenate([kc_ref[0, 0, 2 * g], kc_ref[0, 0, 2 * g + 1]], axis=-1).astype(BF16)
            vcaug_scr[0, g] = jnp.concatenate([vc_ref[0, 0, 2 * g], ones_c], axis=-1).astype(BF16)
            vcaug_scr[1, g] = jnp.concatenate([ones_c, vc_ref[0, 0, 2 * g + 1]], axis=-1).astype(BF16)

    start = pl.multiple_of(_na_union_start(i, rows) * GRID_W, GRID_W)
    lowq = jax.lax.broadcasted_iota(jnp.int32, (NA_Q, 128), 1) < HEAD_DIM
    for g in range(npair):
        lsl = slice(g * 128, (g + 1) * 128)
        qs = q_ref[:, lsl]
        kw = k_ref[pl.ds(start, NA_K), lsl]
        kcx = kcp_scr[g]
        res = []
        for hh in range(2):
            qz = jnp.where(lowq if hh == 0 else jnp.logical_not(lowq), qs, jnp.zeros_like(qs))
            s_w = _nt_dot(qz, kw) + bias_ref[0, 0, 2 * g + hh]
            s_c = _nt_dot(qz, kcx)
            m = jnp.maximum(jnp.max(s_w, axis=-1, keepdims=True), jnp.max(s_c, axis=-1, keepdims=True))
            p_w = jnp.exp2(s_w - m).astype(BF16)
            p_c = jnp.exp2(s_c - m).astype(BF16)
            res.append(jnp.dot(p_w, vaug_scr[hh, g, pl.ds(start, NA_K), :], preferred_element_type=F32)
                       + jnp.dot(p_c, vcaug_scr[hh, g], preferred_element_type=F32))
        num = jnp.where(lowq, res[0], res[1])
        den = pltpu.roll(jnp.where(lowq, res[1], res[0]), HEAD_DIM, 1)
        o_ref[:, lsl] = ((num / den) * za_ref[:, lsl].astype(F32)).astype(BF16)


def _na_attn(q, k, v, cache_k, cache_v, layer, bias, za, seq):
    t = q.shape[0]
    nb = t // seq
    rows = seq // GRID_W
    assert rows % NA_ROWS == 0 and rows >= NA_KROWS
    steps = rows // NA_ROWS
    step_case = _na_cases(rows)[1]
    past = cache_k.shape[3]
    qblk = pl.BlockSpec((NA_Q, WIDTH_A), lambda b, i: (b * steps + i, 0))
    kvblk = pl.BlockSpec((seq, WIDTH_A), lambda b, i: (b, 0))
    cblk = pl.BlockSpec((1, 1, N_HEADS, past, HEAD_DIM), lambda b, i: (b, layer, 0, 0, 0))
    bblk = pl.BlockSpec((1, 1, N_HEADS, NA_Q, NA_K), lambda b, i: (layer, _select_case(i, step_case), 0, 0, 0))
    return pl.pallas_call(
        functools.partial(_na_attn_kernel, rows=rows),
        grid=(nb, steps),
        in_specs=[qblk, kvblk, kvblk, cblk, cblk, bblk, qblk],
        out_specs=qblk,
        out_shape=jax.ShapeDtypeStruct((t, WIDTH_A), BF16),
        scratch_shapes=[
            pltpu.VMEM((2, N_HEADS // 2, seq, 128), BF16),
            pltpu.VMEM((N_HEADS // 2, past, 128), BF16),
            pltpu.VMEM((2, N_HEADS // 2, past, 128), BF16),
        ],
        compiler_params=_params(2),
        name="na_attn",
    )(q, k, v, cache_k, cache_v, bias, za)


def _fourier_kernel(c_ref, s_ref, ucs_ref, wf_ref, zb_ref, o_ref):
    uc = ucs_ref[:, 0:WIDTH_B]
    us = ucs_ref[:, WIDTH_B:2 * WIDTH_B]
    y = (jnp.dot(c_ref[...], uc, preferred_element_type=F32)
         - jnp.dot(s_ref[...], us, preferred_element_type=F32))
    y2 = jnp.dot(y.astype(BF16), wf_ref[0], preferred_element_type=F32)
    o_ref[...] = (y2 * zb_ref[...].astype(F32)).astype(BF16)


def _fourier(ucs, zb, cmat, smat, wf, layer, seq, tk):
    t = ucs.shape[0]
    nb = t // seq
    nk = seq // tk
    return pl.pallas_call(
        _fourier_kernel,
        grid=(nk, nb),
        in_specs=[
            pl.BlockSpec((tk, seq), lambda k, b: (k, 0)),
            pl.BlockSpec((tk, seq), lambda k, b: (k, 0)),
            pl.BlockSpec((seq, 2 * WIDTH_B), lambda k, b: (b, 0)),
            _layer_spec((WIDTH_B, WIDTH_B), layer),
            pl.BlockSpec((tk, WIDTH_B), lambda k, b: (b * nk + k, 0)),
        ],
        out_specs=pl.BlockSpec((tk, WIDTH_B), lambda k, b: (b * nk + k, 0)),
        out_shape=jax.ShapeDtypeStruct((t, WIDTH_B), BF16),
        compiler_params=_params(2),
        name="fourier",
    )(cmat, smat, ucs, wf, zb)


def _dft_consts(seq):
    n = np.arange(seq, dtype=np.int64)
    ang = 2.0 * np.pi * ((n[:, None] * n[None, :]) % seq).astype(np.float64) / seq
    nrm = 1.0 / np.sqrt(float(seq) * GROUP_B)
    cmat = (np.cos(ang) * nrm).astype(np.float32)
    smat = (np.sin(ang) * nrm).astype(np.float32)
    return cmat, smat


def _group_dft_const():
    m = np.arange(GROUP_B, dtype=np.int64)
    ang = 2.0 * np.pi * ((m[:, None] * m[None, :]) % GROUP_B).astype(np.float64) / GROUP_B
    eye = np.eye(N_GROUPS_B)
    return np.concatenate([np.kron(eye, np.cos(ang)), np.kron(eye, np.sin(ang))], axis=1).astype(np.float32)


def _pool_kernel(u_ref, zc_ref, w_ref, ps_ref, o_ref, pad_ref, *, seq):
    halo = POOL_HALO
    pad_ref[0:halo, :] = jnp.zeros((halo, WIDTH_C), F32)
    pad_ref[halo + seq:2 * halo + seq, :] = jnp.zeros((halo, WIDTH_C), F32)
    pad_ref[halo:halo + seq, :] = u_ref[...]
    ext = POOL_TILE + 2 * halo
    lane = jax.lax.broadcasted_iota(jnp.int32, (POOL_TILE, WIDTH_C), 1)
    g0, g1, g2 = lane < GROUP_C, lane < 2 * GROUP_C, lane < 3 * GROUP_C
    half = jnp.where(g0, POOL_WINDOWS[0] // 2,
                     jnp.where(g1, POOL_WINDOWS[1] // 2,
                               jnp.where(g2, POOL_WINDOWS[2] // 2, POOL_WINDOWS[3] // 2)))
    rowi = jax.lax.broadcasted_iota(jnp.int32, (POOL_TILE, WIDTH_C), 0)
    for ti in range(seq // POOL_TILE):
        t0 = ti * POOL_TILE
        x = pad_ref[t0:t0 + ext, :]
        a1 = x + pltpu.roll(x, 1, 0)
        w4 = pltpu.roll(a1, 1, 0) + pltpu.roll(a1, ext - 1, 0)
        w8 = pltpu.roll(w4, 2, 0) + pltpu.roll(w4, ext - 2, 0)
        w16 = pltpu.roll(w8, 4, 0) + pltpu.roll(w8, ext - 4, 0)
        mid = slice(halo, halo + POOL_TILE)
        wsum = jnp.where(g0, a1[mid], jnp.where(g1, w4[mid], jnp.where(g2, w8[mid], w16[mid])))
        tpos = rowi + t0
        cnt = (jnp.minimum(tpos + half, seq) - jnp.maximum(tpos - half, 0)).astype(F32)
        dlt = wsum / cnt - x[mid]
        y = jnp.dot(dlt.astype(BF16), w_ref[...], preferred_element_type=F32) * ps_ref[...]
        o_ref[t0:t0 + POOL_TILE, :] = (y * zc_ref[t0:t0 + POOL_TILE, :].astype(F32)).astype(BF16)


def _pool(uc, zc, wp, ps, seq):
    t = uc.shape[0]
    blk = pl.BlockSpec((seq, WIDTH_C), lambda b: (b, 0))
    return pl.pallas_call(
        functools.partial(_pool_kernel, seq=seq),
        grid=(t // seq,),
        in_specs=[blk, blk, _const_spec((WIDTH_C, WIDTH_C)), _const_spec((1, WIDTH_C))],
        out_specs=blk,
        out_shape=jax.ShapeDtypeStruct((t, WIDTH_C), BF16),
        scratch_shapes=[pltpu.VMEM((seq + 2 * POOL_HALO, WIDTH_C), F32)],
        compiler_params=_params(1),
        name="pool",
    )(uc, zc, wp, ps)


def _merge_kernel(x_ref, mod_ref, ya_ref, yb_ref, yc_ref, ga_ref, gb_ref, gc_ref,
                  pa_ref, pb_ref, pc_ref, wo_ref, o_ref):
    gate = mod_ref[0, 2:3, :]
    m = (ga_ref[...].astype(F32) * jnp.dot(ya_ref[...], pa_ref[0], preferred_element_type=F32)
         + gb_ref[...].astype(F32) * jnp.dot(yb_ref[...], pb_ref[0], preferred_element_type=F32)
         + gc_ref[...].astype(F32) * jnp.dot(yc_ref[...], pc_ref[0], preferred_element_type=F32))
    o_ref[...] = x_ref[...] + gate * jnp.dot(m.astype(BF16), wo_ref[0], preferred_element_type=F32)


def _merge(x2, mod, mod_seq, layer, ya, yb, yc, ga, gb, gc, pa, pb, pc, wo, tm=256):
    t = x2.shape[0]
    tiles_per_seq = mod_seq // tm
    row = lambda w: pl.BlockSpec((tm, w), lambda i: (i, 0))
    return pl.pallas_call(
        _merge_kernel,
        grid=(t // tm,),
        in_specs=[
            row(D_MODEL),
            pl.BlockSpec((1, 3, D_MODEL), lambda i: (i // tiles_per_seq, 0, 0)),
            row(WIDTH_A), row(WIDTH_B), row(WIDTH_C),
            row(D_MODEL), row(D_MODEL), row(D_MODEL),
            _layer_spec((WIDTH_A, D_MODEL), layer), _layer_spec((WIDTH_B, D_MODEL), layer),
            _layer_spec((WIDTH_C, D_MODEL), layer), _layer_spec((D_MODEL, D_MODEL), layer),
        ],
        out_specs=row(D_MODEL),
        out_shape=jax.ShapeDtypeStruct((t, D_MODEL), F32),
        compiler_params=_params(1),
        name="merge",
    )(x2, mod, ya, yb, yc, ga, gb, gc, pa, pb, pc, wo)


def _block_diag(w):
    g, c, e = w.shape
    eye = jnp.eye(g, dtype=w.dtype)
    return (eye[:, None, :, None] * w[:, :, None, :]).reshape(g * c, g * e)


def kernel(x_prompt, x_sample, cache_k, cache_v, c, c_ctx, norm_g, w_ada, b_ada, w_in, q_norm_g, k_norm_g,
           rpb, w_fnet, w_pool, pool_scale, p_a, p_b, p_c, w_o):
    nb_p, seq_p, _ = x_prompt.shape
    nb_s, seq_s, _ = x_sample.shape

    cond = jnp.concatenate([c_ctx[None, :], c, jnp.zeros((8 - 1 - nb_s, D_MODEL), F32)], axis=0)
    ada = _ada(cond, w_ada, b_ada).reshape(DEPTH, 8, 3, D_MODEL)
    bias = _bias_table(rpb, seq_s // GRID_W)

    head_mean = jnp.asarray(np.kron(np.eye(N_HEADS), np.full((HEAD_DIM, HEAD_DIM), 1.0 / HEAD_DIM)), BF16)
    bd = jnp.asarray(_group_dft_const()).astype(BF16)
    dft_p = [jnp.asarray(m).astype(BF16) for m in _dft_consts(seq_p)]
    dft_s = [jnp.asarray(m).astype(BF16) for m in _dft_consts(seq_s)]

    w_in_b, wf_b = w_in.astype(BF16), w_fnet.astype(BF16)
    pa, pb, pc, wo = p_a.astype(BF16), p_b.astype(BF16), p_c.astype(BF16), w_o.astype(BF16)

    xp = x_prompt.reshape(nb_p * seq_p, D_MODEL)
    xs = x_sample.reshape(nb_s * seq_s, D_MODEL)
    new_kv = None
    for l in range(DEPTH):
        ng = norm_g[l][None, :]
        qg = jnp.tile(q_norm_g[l], N_HEADS)[None, :]
        kg = jnp.tile(k_norm_g[l], N_HEADS)[None, :]
        wp = _block_diag(w_pool[l]).astype(BF16)
        ps = pool_scale[l][None, :]
        mod_p = ada[l, 0:1]
        mod_s = ada[l, 1:1 + nb_s]

        (q, k, v, za, ucs, zb, uc, zc, ga, gb, gc, kf, vf) = _inproj(
            xp, mod_p, seq_p, nb_p * seq_p, l, ng, w_in_b, qg, kg, head_mean, bd, prev_kv=new_kv, kv_cache=True)
        new_kv = (kf, vf)
        ya = _ctx_attn(q, k, v, za, seq_p)
        yb = _fourier(ucs, zb, dft_p[0], dft_p[1], wf_b, l, seq_p, tk=seq_p)
        yc = _pool(uc, zc, wp, ps, seq_p)
        xp = _merge(xp, mod_p, nb_p * seq_p, l, ya, yb, yc, ga, gb, gc, pa, pb, pc, wo)

        (q, k, v, za, ucs, zb, uc, zc, ga, gb, gc) = _inproj(
            xs, mod_s, seq_s, seq_s, l, ng, w_in_b, qg, kg, head_mean, bd)
        ya = _na_attn(q, k, v, cache_k, cache_v, l, bias, za, seq_s)
        yb = _fourier(ucs, zb, dft_s[0], dft_s[1], wf_b, l, seq_s, tk=512)
        yc = _pool(uc, zc, wp, ps, seq_s)
        xs = _merge(xs, mod_s, seq_s, l, ya, yb, yc, ga, gb, gc, pa, pb, pc, wo)

    return (xp.reshape(nb_p, seq_p, D_MODEL), xs.reshape(nb_s, seq_s, D_MODEL), new_kv[0], new_kv[1])
```

```python
import functools

import numpy as np
import jax
import jax.numpy as jnp
from jax.experimental import pallas as pl
from jax.experimental.pallas import tpu as pltpu

D_MODEL = 1024
DEPTH = 2
GRID_W = 64
HEAD_DIM = 64
WIDTH_A = 512
N_HEADS = 8
WIN_R = 8
WIN_C = 16
WIDTH_B = 256
N_GROUPS_B = 4
GROUP_B = 64
WIDTH_C = 256
POOL_WINDOWS = (2, 4, 8, 16)
GROUP_C = 64
EPS = 1e-6
NEG_INF = -1e30
LOG2E = 1.4426950408889634
Q_SCALE = HEAD_DIM ** -0.5 * LOG2E
BASE_W = 4 * WIDTH_A + 2 * WIDTH_B + 2 * WIDTH_C
IN_WIDTH = BASE_W + 3 * D_MODEL

OFF_Q, OFF_K, OFF_V, OFF_ZA = 0, 512, 1024, 1536
OFF_UB, OFF_ZB, OFF_UC, OFF_ZC = 2048, 2304, 2560, 2816
assert BASE_W == 3 * D_MODEL

POOL_HALO = 16
POOL_TILE = 128
ROW_TILE = 512
VMEM_LIMIT = 56 * 1024 * 1024

BF16 = jnp.bfloat16
F32 = jnp.float32


def _params(n_axes, vmem=VMEM_LIMIT):
    return pltpu.CompilerParams(dimension_semantics=("arbitrary",) * n_axes, vmem_limit_bytes=vmem)


def _const_spec(shape):
    zeros = (0,) * len(shape)
    return pl.BlockSpec(shape, lambda *_: zeros)


def _sigmoid(x):
    return 1.0 / (1.0 + jnp.exp(-x))


def _ada_kernel(cond_ref, w_ref, b_ref, o_ref):
    cnd = cond_ref[...]
    a = cnd * _sigmoid(cnd)
    o_ref[0] = jnp.dot(a, w_ref[0], preferred_element_type=F32) + b_ref[0]


def _ada(cond, w_ada, b_ada):
    tn = 1024
    return pl.pallas_call(
        _ada_kernel,
        grid=(DEPTH, 3 * D_MODEL // tn),
        in_specs=[
            pl.BlockSpec((8, D_MODEL), lambda l, j: (0, 0)),
            pl.BlockSpec((1, D_MODEL, tn), lambda l, j: (l, 0, j)),
            pl.BlockSpec((1, 1, tn), lambda l, j: (l, 0, j)),
        ],
        out_specs=pl.BlockSpec((1, 8, tn), lambda l, j: (l, 0, j)),
        out_shape=jax.ShapeDtypeStruct((DEPTH, 8, 3 * D_MODEL), F32),
        compiler_params=_params(2),
        name="ada",
    )(cond, w_ada, b_ada.reshape(DEPTH, 1, 3 * D_MODEL))


def _modulated_norm(x, mod_ref, ng_ref):
    shift = mod_ref[0, 0:1, :]
    scale = mod_ref[0, 1:2, :]
    y = x * jax.lax.rsqrt(jnp.mean(x * x, axis=-1, keepdims=True) + EPS)
    return ((y * ng_ref[...]) * (1.0 + scale) + shift).astype(BF16)


def _inproj_kernel(x_ref, mod_ref, ng_ref, w_ref, qg_ref, kg_ref, hm_ref, bd_ref, *rest, layer, kv_cache):
    n_prev = 2 if (kv_cache and layer > 0) else 0
    prev, outs = rest[:n_prev], rest[n_prev:]
    if kv_cache:
        (q_o, k_o, v_o, za_o, ucs_o, zb_o, uc_o, zc_o, kf_o, vf_o) = outs
    else:
        (q_o, k_o, v_o, za_o, ucs_o, zb_o, uc_o, zc_o) = outs
    hb = _modulated_norm(x_ref[...], mod_ref, ng_ref)

    def proj(off, width):
        return jnp.dot(hb, w_ref[0, :, off:off + width], preferred_element_type=F32)

    def head_norm(t, g_ref):
        ms = jnp.dot((t * t).astype(BF16), hm_ref[...], preferred_element_type=F32)
        return t * jax.lax.rsqrt(ms + EPS) * g_ref[...]

    q = head_norm(proj(OFF_Q, WIDTH_A), qg_ref)
    q_o[...] = (q * Q_SCALE).astype(BF16)
    k = head_norm(proj(OFF_K, WIDTH_A), kg_ref)
    k_o[...] = k.astype(BF16)
    v = proj(OFF_V, WIDTH_A)
    v_o[...] = v.astype(BF16)
    if kv_cache:
        seq = kf_o.shape[3]
        if layer > 0:
            kf_o[:, 0:layer] = prev[0][...]
            vf_o[:, 0:layer] = prev[1][...]
        for si in range(kf_o.shape[0]):
            for hh in range(N_HEADS):
                rsl, csl = slice(si * seq, (si + 1) * seq), slice(hh * HEAD_DIM, (hh + 1) * HEAD_DIM)
                kf_o[si, layer, hh] = k[rsl, csl]
                vf_o[si, layer, hh] = v[rsl, csl]
    za = proj(OFF_ZA, WIDTH_A)
    za_o[...] = (za * _sigmoid(za)).astype(BF16)
    ub = proj(OFF_UB, WIDTH_B)
    ucs_o[...] = jnp.dot(ub.astype(BF16), bd_ref[...], preferred_element_type=F32).astype(BF16)
    zb = proj(OFF_ZB, WIDTH_B)
    zb_o[...] = (zb * _sigmoid(zb)).astype(BF16)
    uc_o[...] = proj(OFF_UC, WIDTH_C)
    zc = proj(OFF_ZC, WIDTH_C)
    zc_o[...] = (zc * _sigmoid(zc)).astype(BF16)


def _layer_spec(shape, layer):
    zeros = (0,) * len(shape)
    return pl.BlockSpec((1,) + tuple(shape), lambda *_: (layer,) + zeros)


def _inproj(x2, mod, seq, mod_seq, layer, ng, w_in, qg, kg, hm, bd, prev_kv=None, kv_cache=False, tm=ROW_TILE):
    t = x2.shape[0]
    tiles_per_mod = mod_seq // tm
    row = lambda w: pl.BlockSpec((tm, w), lambda i: (i, 0))
    widths = [(WIDTH_A, BF16)] * 4 + [(2 * WIDTH_B, BF16), (WIDTH_B, BF16), (WIDTH_C, F32), (WIDTH_C, BF16)]
    in_specs = [
        row(D_MODEL),
        pl.BlockSpec((1, 3, D_MODEL), lambda i: (i // tiles_per_mod, 0, 0)),
        _const_spec((1, D_MODEL)),
        pl.BlockSpec((1, D_MODEL, BASE_W), lambda i: (layer, 0, 0)),
        _const_spec((1, WIDTH_A)),
        _const_spec((1, WIDTH_A)),
        _const_spec((WIDTH_A, WIDTH_A)),
        _const_spec((WIDTH_B, 2 * WIDTH_B)),
    ]
    args = [x2, mod, ng, w_in, qg, kg, hm, bd]
    out_specs = [row(w) for w, _ in widths]
    out_shape = [jax.ShapeDtypeStruct((t, w), dt) for w, dt in widths]
    if kv_cache:
        assert tm % seq == 0
        cache = lambda n: pl.BlockSpec((tm // seq, n, N_HEADS, seq, HEAD_DIM), lambda i: (i, 0, 0, 0, 0))
        if layer > 0:
            in_specs += [cache(layer)] * 2
            args += list(prev_kv)
        out_specs += [cache(layer + 1)] * 2
        out_shape += [jax.ShapeDtypeStruct((t // seq, layer + 1, N_HEADS, seq, HEAD_DIM), F32)] * 2
    return pl.pallas_call(
        functools.partial(_inproj_kernel, layer=layer, kv_cache=kv_cache),
        grid=(t // tm,),
        in_specs=in_specs,
        out_specs=out_specs,
        out_shape=out_shape,
        compiler_params=_params(1),
        name="inproj",
    )(*args)


def _ctx_attn_kernel(q_ref, k_ref, v_ref, za_ref, o_ref):
    outs = []
    for hh in range(N_HEADS):
        sl = slice(hh * HEAD_DIM, (hh + 1) * HEAD_DIM)
        qh = q_ref[:, sl]
        kh = k_ref[:, sl]
        vh = v_ref[:, sl]
        s = jax.lax.dot_general(qh, kh, (((1,), (1,)), ((), ())), preferred_element_type=F32)
        m = jnp.max(s, axis=-1, keepdims=True)
        p = jnp.exp2(s - m)
        l = jnp.sum(p, axis=-1, keepdims=True)
        o = jnp.dot(p.astype(BF16), vh, preferred_element_type=F32)
        outs.append(o / l)
    ya = jnp.concatenate(outs, axis=-1)
    o_ref[...] = (ya * za_ref[...].astype(F32)).astype(BF16)


def _ctx_attn(q, k, v, za, seq):
    t = q.shape[0]
    blk = pl.BlockSpec((seq, WIDTH_A), lambda b: (b, 0))
    return pl.pallas_call(
        _ctx_attn_kernel,
        grid=(t // seq,),
        in_specs=[blk, blk, blk, blk],
        out_specs=blk,
        out_shape=jax.ShapeDtypeStruct((t, WIDTH_A), BF16),
        compiler_params=_params(1),
        name="ctx_attn",
    )(q, k, v, za)


NA_ROWS = 4
NA_KROWS = WIN_R + NA_ROWS - 1
NA_Q = NA_ROWS * GRID_W
NA_K = NA_KROWS * GRID_W


def _na_union_start(i, rows):
    return jnp.clip(NA_ROWS * i - WIN_R // 2, 0, rows - NA_KROWS)


def _na_cases(rows):
    cases, step_case = [], []
    for i in range(rows // NA_ROWS):
        u0 = min(max(NA_ROWS * i - WIN_R // 2, 0), rows - NA_KROWS)
        los = tuple(min(max(NA_ROWS * i + j - WIN_R // 2, 0), rows - WIN_R) - u0 for j in range(NA_ROWS))
        case = (u0 - NA_ROWS * i + WIN_R - 1, los)
        if case not in cases:
            cases.append(case)
        step_case.append(cases.index(case))
    return cases, step_case


def _select_case(idx, values):
    out = values[-1]
    for t in range(len(values) - 2, -1, -1):
        out = jnp.where(idx == t, values[t], out)
    return out


def _bias_kernel(src_ref, o_ref, *, rows):
    ty = pl.program_id(1)
    cases, _ = _na_cases(rows)
    c = _select_case(ty, [cs_[0] for cs_ in cases])
    qc = jax.lax.broadcasted_iota(jnp.int32, (GRID_W, 128), 0)
    lane = jax.lax.broadcasted_iota(jnp.int32, (GRID_W, 128), 1)
    upper = lane >= GRID_W
    kc = jnp.where(upper, lane - GRID_W, lane)
    cs = jnp.clip(qc - WIN_C // 2, 0, GRID_W - WIN_C)
    col_ok = (kc >= cs) & (kc < cs + WIN_C)
    for hh in range(N_HEADS):
        for j in range(NA_ROWS):
            lo = _select_case(ty, [cs_[1][j] for cs_ in cases])
            for p in range((NA_KROWS + 1) // 2):
                sidx = jnp.clip(2 * p - j + c + 1, 0, 15)
                row = src_ref[0, hh, pl.ds(sidx, 1), :]
                t = pltpu.roll(jnp.broadcast_to(row, (GRID_W, 128)), 0, 1, stride=1, stride_axis=0)
                kk = jnp.where(upper, 2 * p + 1, 2 * p)
                ok = col_ok & (kk >= lo) & (kk < lo + WIN_R)
                val = jnp.where(ok, t * LOG2E, NEG_INF)
                rsl = slice(j * GRID_W, (j + 1) * GRID_W)
                if 2 * p + 1 < NA_KROWS:
                    o_ref[0, 0, hh, rsl, p * 128:(p + 1) * 128] = val
                else:
                    o_ref[0, 0, hh, rsl, p * 128:p * 128 + GRID_W] = val[:, :GRID_W]


def _bias_table(rpb, rows):
    nr = 2 * WIN_R - 1
    pad = jnp.pad(rpb, ((0, 0), (0, 0), (1, 1), (0, 0)))
    lo_half, hi_half = pad[:, :, 0:nr + 1], pad[:, :, 1:nr + 2]
    src = jnp.zeros((DEPTH, N_HEADS, nr + 1, 128), F32)
    src = src.at[..., 0:WIN_C].set(lo_half[..., WIN_C - 1:])
    src = src.at[..., 128 - (WIN_C - 1):].set(lo_half[..., :WIN_C - 1])
    src = src.at[..., GRID_W - (WIN_C - 1):GRID_W + WIN_C].set(hi_half)
    ncase = len(_na_cases(rows)[0])
    return pl.pallas_call(
        functools.partial(_bias_kernel, rows=rows),
        grid=(DEPTH, ncase),
        in_specs=[pl.BlockSpec((1, N_HEADS, nr + 1, 128), lambda l, ty: (l, 0, 0, 0))],
        out_specs=pl.BlockSpec((1, 1, N_HEADS, NA_Q, NA_K), lambda l, ty: (l, ty, 0, 0, 0)),
        out_shape=jax.ShapeDtypeStruct((DEPTH, ncase, N_HEADS, NA_Q, NA_K), F32),
        compiler_params=_params(2),
        name="rpb_table",
    )(src)


def _nt_dot(a, b):
    return jax.lax.dot_general(a, b, (((1,), (1,)), ((), ())), preferred_element_type=F32)


def _na_attn_kernel(q_ref, k_ref, v_ref, kc_ref, vc_ref, bias_ref, za_ref, o_ref,
                    vaug_scr, kcp_scr, vcaug_scr, *, rows):
    i = pl.program_id(1)
    npair = N_HEADS // 2
    past = kc_ref.shape[3]

    @pl.when(i == 0)
    def _prepare():
        low = jax.lax.broadcasted_iota(jnp.int32, (v_ref.shape[0], 128), 1) < HEAD_DIM
        ones_c = jnp.ones((past, HEAD_DIM), F32)
        for g in range(npair):
            vs = v_ref[:, g * 128:(g + 1) * 128]
            vaug_scr[0, g] = jnp.where(low, vs, jnp.ones_like(vs))
            vaug_scr[1, g] = jnp.where(low, jnp.ones_like(vs), vs)
            kcp_scr[g] = jnp.concatenate([kc_ref[0, 0, 2 * g], kc_ref[0, 0, 2 * g + 1]], axis=-1).astype(BF16)
            vcaug_scr[0, g] = jnp.concatenate([vc_ref[0, 0, 2 * g], ones_c], axis=-1).astype(BF16)
            vcaug_scr[1, g] = jnp.concatenate([ones_c, vc_ref[0, 0, 2 * g + 1]], axis=-1).astype(BF16)

    start = pl.multiple_of(_na_union_start(i, rows) * GRID_W, GRID_W)
    lowq = jax.lax.broadcasted_iota(jnp.int32, (NA_Q, 128), 1) < HEAD_DIM
    for g in range(npair):
        lsl = slice(g * 128, (g + 1) * 128)
        qs = q_ref[:, lsl]
        kw = k_ref[pl.ds(start, NA_K), lsl]
        kcx = kcp_scr[g]
        res = []
        for hh in range(2):
            qz = jnp.where(lowq if hh == 0 else jnp.logical_not(lowq), qs, jnp.zeros_like(qs))
            s_w = _nt_dot(qz, kw) + bias_ref[0, 0, 2 * g + hh]
            s_c = _nt_dot(qz, kcx)
            m = jnp.maximum(jnp.max(s_w, axis=-1, keepdims=True), jnp.max(s_c, axis=-1, keepdims=True))
            p_w = jnp.exp2(s_w - m).astype(BF16)
            p_c = jnp.exp2(s_c - m).astype(BF16)
            res.append(jnp.dot(p_w, vaug_scr[hh, g, pl.ds(start, NA_K), :], preferred_element_type=F32)
                       + jnp.dot(p_c, vcaug_scr[hh, g], preferred_element_type=F32))
        num = jnp.where(lowq, res[0], res[1])
        den = pltpu.roll(jnp.where(lowq, res[1], res[0]), HEAD_DIM, 1)
        o_ref[:, lsl] = ((num / den) * za_ref[:, lsl].astype(F32)).astype(BF16)


def _na_attn(q, k, v, cache_k, cache_v, layer, bias, za, seq):
    t = q.shape[0]
    nb = t // seq
    rows = seq // GRID_W
    assert rows % NA_ROWS == 0 and rows >= NA_KROWS
    steps = rows // NA_ROWS
    step_case = _na_cases(rows)[1]
    past = cache_k.shape[3]
    qblk = pl.BlockSpec((NA_Q, WIDTH_A), lambda b, i: (b * steps + i, 0))
    kvblk = pl.BlockSpec((seq, WIDTH_A), lambda b, i: (b, 0))
    cblk = pl.BlockSpec((1, 1, N_HEADS, past, HEAD_DIM), lambda b, i: (b, layer, 0, 0, 0))
    bblk = pl.BlockSpec((1, 1, N_HEADS, NA_Q, NA_K), lambda b, i: (layer, _select_case(i, step_case), 0, 0, 0))
    return pl.pallas_call(
        functools.partial(_na_attn_kernel, rows=rows),
        grid=(nb, steps),
        in_specs=[qblk, kvblk, kvblk, cblk, cblk, bblk, qblk],
        out_specs=qblk,
        out_shape=jax.ShapeDtypeStruct((t, WIDTH_A), BF16),
        scratch_shapes=[
            pltpu.VMEM((2, N_HEADS // 2, seq, 128), BF16),
            pltpu.VMEM((N_HEADS // 2, past, 128), BF16),
            pltpu.VMEM((2, N_HEADS // 2, past, 128), BF16),
        ],
        compiler_params=_params(2),
        name="na_attn",
    )(q, k, v, cache_k, cache_v, bias, za)


def _fourier_kernel(c_ref, s_ref, ucs_ref, wf_ref, zb_ref, o_ref):
    uc = ucs_ref[:, 0:WIDTH_B]
    us = ucs_ref[:, WIDTH_B:2 * WIDTH_B]
    y = (jnp.dot(c_ref[...], uc, preferred_element_type=F32)
         - jnp.dot(s_ref[...], us, preferred_element_type=F32))
    y2 = jnp.dot(y.astype(BF16), wf_ref[0], preferred_element_type=F32)
    o_ref[...] = (y2 * zb_ref[...].astype(F32)).astype(BF16)


def _fourier(ucs, zb, cmat, smat, wf, layer, seq, tk):
    t = ucs.shape[0]
    nb = t // seq
    nk = seq // tk
    return pl.pallas_call(
        _fourier_kernel,
        grid=(nk, nb),
        in_specs=[
            pl.BlockSpec((tk, seq), lambda k, b: (k, 0)),
            pl.BlockSpec((tk, seq), lambda k, b: (k, 0)),
            pl.BlockSpec((seq, 2 * WIDTH_B), lambda k, b: (b, 0)),
            _layer_spec((WIDTH_B, WIDTH_B), layer),
            pl.BlockSpec((tk, WIDTH_B), lambda k, b: (b * nk + k, 0)),
        ],
        out_specs=pl.BlockSpec((tk, WIDTH_B), lambda k, b: (b * nk + k, 0)),
        out_shape=jax.ShapeDtypeStruct((t, WIDTH_B), BF16),
        compiler_params=_params(2),
        name="fourier",
    )(cmat, smat, ucs, wf, zb)


def _dft_consts(seq):
    n = np.arange(seq, dtype=np.int64)
    ang = 2.0 * np.pi * ((n[:, None] * n[None, :]) % seq).astype(np.float64) / seq
    nrm = 1.0 / np.sqrt(float(seq) * GROUP_B)
    cmat = (np.cos(ang) * nrm).astype(np.float32)
    smat = (np.sin(ang) * nrm).astype(np.float32)
    return cmat, smat


def _group_dft_const():
    m = np.arange(GROUP_B, dtype=np.int64)
    ang = 2.0 * np.pi * ((m[:, None] * m[None, :]) % GROUP_B).astype(np.float64) / GROUP_B
    eye = np.eye(N_GROUPS_B)
    return np.concatenate([np.kron(eye, np.cos(ang)), np.kron(eye, np.sin(ang))], axis=1).astype(np.float32)


def _pool_kernel(u_ref, zc_ref, w_ref, ps_ref, o_ref, pad_ref, *, seq):
    halo = POOL_HALO
    pad_ref[0:halo, :] = jnp.zeros((halo, WIDTH_C), F32)
    pad_ref[halo + seq:2 * halo + seq, :] = jnp.zeros((halo, WIDTH_C), F32)
    pad_ref[halo:halo + seq, :] = u_ref[...]
    ext = POOL_TILE + 2 * halo
    lane = jax.lax.broadcasted_iota(jnp.int32, (POOL_TILE, WIDTH_C), 1)
    g0, g1, g2 = lane < GROUP_C, lane < 2 * GROUP_C, lane < 3 * GROUP_C
    half = jnp.where(g0, POOL_WINDOWS[0] // 2,
                     jnp.where(g1, POOL_WINDOWS[1] // 2,
                               jnp.where(g2, POOL_WINDOWS[2] // 2, POOL_WINDOWS[3] // 2)))
    rowi = jax.lax.broadcasted_iota(jnp.int32, (POOL_TILE, WIDTH_C), 0)
    for ti in range(seq // POOL_TILE):
        t0 = ti * POOL_TILE
        x = pad_ref[t0:t0 + ext, :]
        a1 = x + pltpu.roll(x, 1, 0)
        w4 = pltpu.roll(a1, 1, 0) + pltpu.roll(a1, ext - 1, 0)
        w8 = pltpu.roll(w4, 2, 0) + pltpu.roll(w4, ext - 2, 0)
        w16 = pltpu.roll(w8, 4, 0) + pltpu.roll(w8, ext - 4, 0)
        mid = slice(halo, halo + POOL_TILE)
        wsum = jnp.where(g0, a1[mid], jnp.where(g1, w4[mid], jnp.where(g2, w8[mid], w16[mid])))
        tpos = rowi + t0
        cnt = (jnp.minimum(tpos + half, seq) - jnp.maximum(tpos - half, 0)).astype(F32)
        dlt = wsum / cnt - x[mid]
        y = jnp.dot(dlt.astype(BF16), w_ref[...], preferred_element_type=F32) * ps_ref[...]
        o_ref[t0:t0 + POOL_TILE, :] = (y * zc_ref[t0:t0 + POOL_TILE, :].astype(F32)).astype(BF16)


def _pool(uc, zc, wp, ps, seq):
    t = uc.shape[0]
    blk = pl.BlockSpec((seq, WIDTH_C), lambda b: (b, 0))
    return pl.pallas_call(
        functools.partial(_pool_kernel, seq=seq),
        grid=(t // seq,),
        in_specs=[blk, blk, _const_spec((WIDTH_C, WIDTH_C)), _const_spec((1, WIDTH_C))],
        out_specs=blk,
        out_shape=jax.ShapeDtypeStruct((t, WIDTH_C), BF16),
        scratch_shapes=[pltpu.VMEM((seq + 2 * POOL_HALO, WIDTH_C), F32)],
        compiler_params=_params(1),
        name="pool",
    )(uc, zc, wp, ps)


def _merge_kernel(x_ref, mod_ref, ng_ref, wg_ref, ya_ref, yb_ref, yc_ref,
                  pa_ref, pb_ref, pc_ref, wo_ref, o_ref):
    x = x_ref[...]
    hb = _modulated_norm(x, mod_ref, ng_ref)
    gate = mod_ref[0, 2:3, :]

    def gated(y_ref, p_ref, col):
        g = jnp.dot(hb, wg_ref[0, :, col * D_MODEL:(col + 1) * D_MODEL], preferred_element_type=F32)
        return _sigmoid(g) * jnp.dot(y_ref[...], p_ref[0], preferred_element_type=F32)

    m = gated(ya_ref, pa_ref, 0) + gated(yb_ref, pb_ref, 1) + gated(yc_ref, pc_ref, 2)
    o_ref[...] = x + gate * jnp.dot(m.astype(BF16), wo_ref[0], preferred_element_type=F32)


def _merge(x2, mod, mod_seq, layer, ng, w_in, ya, yb, yc, pa, pb, pc, wo, tm=ROW_TILE):
    t = x2.shape[0]
    tiles_per_seq = mod_seq // tm
    row = lambda w: pl.BlockSpec((tm, w), lambda i: (i, 0))
    return pl.pallas_call(
        _merge_kernel,
        grid=(t // tm,),
        in_specs=[
            row(D_MODEL),
            pl.BlockSpec((1, 3, D_MODEL), lambda i: (i // tiles_per_seq, 0, 0)),
            _const_spec((1, D_MODEL)),
            pl.BlockSpec((1, D_MODEL, 3 * D_MODEL), lambda i: (layer, 0, 1)),
            row(WIDTH_A), row(WIDTH_B), row(WIDTH_C),
            _layer_spec((WIDTH_A, D_MODEL), layer), _layer_spec((WIDTH_B, D_MODEL), layer),
            _layer_spec((WIDTH_C, D_MODEL), layer), _layer_spec((D_MODEL, D_MODEL), layer),
        ],
        out_specs=row(D_MODEL),
        out_shape=jax.ShapeDtypeStruct((t, D_MODEL), F32),
        compiler_params=_params(1),
        name="merge",
    )(x2, mod, ng, w_in, ya, yb, yc, pa, pb, pc, wo)


def _block_diag(w):
    g, c, e = w.shape
    eye = jnp.eye(g, dtype=w.dtype)
    return (eye[:, None, :, None] * w[:, :, None, :]).reshape(g * c, g * e)


def kernel(x_prompt, x_sample, cache_k, cache_v, c, c_ctx, norm_g, w_ada, b_ada, w_in, q_norm_g, k_norm_g,
           rpb, w_fnet, w_pool, pool_scale, p_a, p_b, p_c, w_o):
    nb_p, seq_p, _ = x_prompt.shape
    nb_s, seq_s, _ = x_sample.shape

    cond = jnp.concatenate([c_ctx[None, :], c, jnp.zeros((8 - 1 - nb_s, D_MODEL), F32)], axis=0)
    ada = _ada(cond, w_ada, b_ada).reshape(DEPTH, 8, 3, D_MODEL)
    bias = _bias_table(rpb, seq_s // GRID_W)

    head_mean = jnp.asarray(np.kron(np.eye(N_HEADS), np.full((HEAD_DIM, HEAD_DIM), 1.0 / HEAD_DIM)), BF16)
    bd = jnp.asarray(_group_dft_const()).astype(BF16)
    dft_p = [jnp.asarray(m).astype(BF16) for m in _dft_consts(seq_p)]
    dft_s = [jnp.asarray(m).astype(BF16) for m in _dft_consts(seq_s)]

    w_in_b, wf_b = w_in.astype(BF16), w_fnet.astype(BF16)
    pa, pb, pc, wo = p_a.astype(BF16), p_b.astype(BF16), p_c.astype(BF16), w_o.astype(BF16)

    xp = x_prompt.reshape(nb_p * seq_p, D_MODEL)
    xs = x_sample.reshape(nb_s * seq_s, D_MODEL)
    new_kv = None
    for l in range(DEPTH):
        ng = norm_g[l][None, :]
        qg = jnp.tile(q_norm_g[l], N_HEADS)[None, :]
        kg = jnp.tile(k_norm_g[l], N_HEADS)[None, :]
        wp = _block_diag(w_pool[l]).astype(BF16)
        ps = pool_scale[l][None, :]
        mod_p = ada[l, 0:1]
        mod_s = ada[l, 1:1 + nb_s]

        (q, k, v, za, ucs, zb, uc, zc, kf, vf) = _inproj(
            xp, mod_p, seq_p, nb_p * seq_p, l, ng, w_in_b, qg, kg, head_mean, bd, prev_kv=new_kv, kv_cache=True)
        new_kv = (kf, vf)
        ya = _ctx_attn(q, k, v, za, seq_p)
        yb = _fourier(ucs, zb, dft_p[0], dft_p[1], wf_b, l, seq_p, tk=seq_p)
        yc = _pool(uc, zc, wp, ps, seq_p)
        xp = _merge(xp, mod_p, nb_p * seq_p, l, ng, w_in_b, ya, yb, yc, pa, pb, pc, wo)

        (q, k, v, za, ucs, zb, uc, zc) = _inproj(
            xs, mod_s, seq_s, seq_s, l, ng, w_in_b, qg, kg, head_mean, bd)
        ya = _na_attn(q, k, v, cache_k, cache_v, l, bias, za, seq_s)
        yb = _fourier(ucs, zb, dft_s[0], dft_s[1], wf_b, l, seq_s, tk=512)
        yc = _pool(uc, zc, wp, ps, seq_s)
        xs = _merge(xs, mod_s, seq_s, l, ng, w_in_b, ya, yb, yc, pa, pb, pc, wo)

    return (xp.reshape(nb_p, seq_p, D_MODEL), xs.reshape(nb_s, seq_s, D_MODEL), new_kv[0], new_kv[1])
```

```python
import functools

import numpy as np
import jax
import jax.numpy as jnp
from jax.experimental import pallas as pl
from jax.experimental.pallas import tpu as pltpu

D_MODEL = 1024
DEPTH = 2
GRID_W = 64
HEAD_DIM = 64
WIDTH_A = 512
N_HEADS = 8
WIN_R = 8
WIN_C = 16
WIDTH_B = 256
N_GROUPS_B = 4
GROUP_B = 64
WIDTH_C = 256
POOL_WINDOWS = (2, 4, 8, 16)
GROUP_C = 64
EPS = 1e-6
NEG_INF = -1e30
LOG2E = 1.4426950408889634
Q_SCALE = HEAD_DIM ** -0.5 * LOG2E
BASE_W = 4 * WIDTH_A + 2 * WIDTH_B + 2 * WIDTH_C
IN_WIDTH = BASE_W + 3 * D_MODEL

OFF_Q, OFF_K, OFF_V, OFF_ZA = 0, 512, 1024, 1536
OFF_UB, OFF_ZB, OFF_UC, OFF_ZC = 2048, 2304, 2560, 2816
assert BASE_W == 3 * D_MODEL

POOL_HALO = 16
POOL_TILE = 128
ROW_TILE = 512
VMEM_LIMIT = 56 * 1024 * 1024

BF16 = jnp.bfloat16
F32 = jnp.float32


def _params(n_axes, vmem=VMEM_LIMIT):
    return pltpu.CompilerParams(dimension_semantics=("arbitrary",) * n_axes, vmem_limit_bytes=vmem)


def _const_spec(shape):
    zeros = (0,) * len(shape)
    return pl.BlockSpec(shape, lambda *_: zeros)


def _sigmoid(x):
    return 1.0 / (1.0 + jnp.exp(-x))


def _ada_kernel(cond_ref, w_ref, b_ref, o_ref):
    cnd = cond_ref[...]
    a = cnd * _sigmoid(cnd)
    o_ref[0] = jnp.dot(a, w_ref[0], preferred_element_type=F32) + b_ref[0]


def _ada(cond, w_ada, b_ada):
    tn = 1024
    return pl.pallas_call(
        _ada_kernel,
        grid=(DEPTH, 3 * D_MODEL // tn),
        in_specs=[
            pl.BlockSpec((8, D_MODEL), lambda l, j: (0, 0)),
            pl.BlockSpec((1, D_MODEL, tn), lambda l, j: (l, 0, j)),
            pl.BlockSpec((1, 1, tn), lambda l, j: (l, 0, j)),
        ],
        out_specs=pl.BlockSpec((1, 8, tn), lambda l, j: (l, 0, j)),
        out_shape=jax.ShapeDtypeStruct((DEPTH, 8, 3 * D_MODEL), F32),
        compiler_params=_params(2),
        name="ada",
    )(cond, w_ada, b_ada.reshape(DEPTH, 1, 3 * D_MODEL))


def _modulated_norm(x, mod_ref, ng_ref):
    shift = mod_ref[0, 0:1, :]
    scale = mod_ref[0, 1:2, :]
    y = x * jax.lax.rsqrt(jnp.mean(x * x, axis=-1, keepdims=True) + EPS)
    return ((y * ng_ref[...]) * (1.0 + scale) + shift).astype(BF16)


def _inproj_kernel(x_ref, mod_ref, ng_ref, w_ref, qg_ref, kg_ref, hm_ref, bd_ref, *rest, layer, kv_cache):
    n_prev = 2 if (kv_cache and layer > 0) else 0
    prev, outs = rest[:n_prev], rest[n_prev:]
    if kv_cache:
        (q_o, k_o, v_o, za_o, ucs_o, zb_o, uc_o, zc_o, kf_o, vf_o) = outs
    else:
        (q_o, k_o, v_o, za_o, ucs_o, zb_o, uc_o, zc_o) = outs
    hb = _modulated_norm(x_ref[...], mod_ref, ng_ref)

    def proj(off, width):
        return jnp.dot(hb, w_ref[0, :, off:off + width], preferred_element_type=F32)

    def head_norm(t, g_ref):
        ms = jnp.dot((t * t).astype(BF16), hm_ref[...], preferred_element_type=F32)
        return t * jax.lax.rsqrt(ms + EPS) * g_ref[...]

    q = head_norm(proj(OFF_Q, WIDTH_A), qg_ref)
    q_o[...] = (q * Q_SCALE).astype(BF16)
    k = head_norm(proj(OFF_K, WIDTH_A), kg_ref)
    k_o[...] = k.astype(BF16)
    v = proj(OFF_V, WIDTH_A)
    v_o[...] = v.astype(BF16)
    if kv_cache:
        seq = kf_o.shape[3]
        if layer > 0:
            kf_o[:, 0:layer] = prev[0][...]
            vf_o[:, 0:layer] = prev[1][...]
        for si in range(kf_o.shape[0]):
            for hh in range(N_HEADS):
                rsl, csl = slice(si * seq, (si + 1) * seq), slice(hh * HEAD_DIM, (hh + 1) * HEAD_DIM)
                kf_o[si, layer, hh] = k[rsl, csl]
                vf_o[si, layer, hh] = v[rsl, csl]
    za = proj(OFF_ZA, WIDTH_A)
    za_o[...] = (za * _sigmoid(za)).astype(BF16)
    ub = proj(OFF_UB, WIDTH_B)
    ucs_o[...] = jnp.dot(ub.astype(BF16), bd_ref[...], preferred_element_type=F32).astype(BF16)
    zb = proj(OFF_ZB, WIDTH_B)
    zb_o[...] = (zb * _sigmoid(zb)).astype(BF16)
    uc_o[...] = proj(OFF_UC, WIDTH_C)
    zc = proj(OFF_ZC, WIDTH_C)
    zc_o[...] = (zc * _sigmoid(zc)).astype(BF16)


def _layer_spec(shape, layer):
    zeros = (0,) * len(shape)
    return pl.BlockSpec((1,) + tuple(shape), lambda *_: (layer,) + zeros)


def _inproj(x2, mod, seq, mod_seq, layer, ng, w_in, qg, kg, hm, bd, prev_kv=None, kv_cache=False, tm=ROW_TILE):
    t = x2.shape[0]
    tiles_per_mod = mod_seq // tm
    row = lambda w: pl.BlockSpec((tm, w), lambda i: (i, 0))
    widths = [(WIDTH_A, BF16)] * 4 + [(2 * WIDTH_B, BF16), (WIDTH_B, BF16), (WIDTH_C, F32), (WIDTH_C, BF16)]
    in_specs = [
        row(D_MODEL),
        pl.BlockSpec((1, 3, D_MODEL), lambda i: (i // tiles_per_mod, 0, 0)),
        _const_spec((1, D_MODEL)),
        pl.BlockSpec((1, D_MODEL, BASE_W), lambda i: (layer, 0, 0)),
        _const_spec((1, WIDTH_A)),
        _const_spec((1, WIDTH_A)),
        _const_spec((WIDTH_A, WIDTH_A)),
        _const_spec((WIDTH_B, 2 * WIDTH_B)),
    ]
    args = [x2, mod, ng, w_in, qg, kg, hm, bd]
    out_specs = [row(w) for w, _ in widths]
    out_shape = [jax.ShapeDtypeStruct((t, w), dt) for w, dt in widths]
    if kv_cache:
        assert tm % seq == 0
        cache = lambda n: pl.BlockSpec((tm // seq, n, N_HEADS, seq, HEAD_DIM), lambda i: (i, 0, 0, 0, 0))
        if layer > 0:
            in_specs += [cache(layer)] * 2
            args += list(prev_kv)
        out_specs += [cache(layer + 1)] * 2
        out_shape += [jax.ShapeDtypeStruct((t // seq, layer + 1, N_HEADS, seq, HEAD_DIM), F32)] * 2
    return pl.pallas_call(
        functools.partial(_inproj_kernel, layer=layer, kv_cache=kv_cache),
        grid=(t // tm,),
        in_specs=in_specs,
        out_specs=out_specs,
        out_shape=out_shape,
        compiler_params=_params(1),
        name="inproj",
    )(*args)


def _nt_dot(a, b):
    return jax.lax.dot_general(a, b, (((1,), (1,)), ((), ())), preferred_element_type=F32)


def _pair_normalise(res, low):
    num = jnp.where(low, res[0], res[1])
    den = pltpu.roll(jnp.where(low, res[1], res[0]), HEAD_DIM, 1)
    return num / den


def _ctx_attention(q_ref, k_ref, v_ref, za_ref, o_ref):
    low = jax.lax.broadcasted_iota(jnp.int32, (q_ref.shape[0], 128), 1) < HEAD_DIM
    for g in range(N_HEADS // 2):
        lsl = slice(g * 128, (g + 1) * 128)
        qs, ks, vs = q_ref[:, lsl], k_ref[:, lsl], v_ref[:, lsl]
        res = []
        for hh in range(2):
            own = low if hh == 0 else jnp.logical_not(low)
            s = _nt_dot(jnp.where(own, qs, jnp.zeros_like(qs)), ks)
            p = jnp.exp2(s - jnp.max(s, axis=-1, keepdims=True)).astype(BF16)
            res.append(jnp.dot(p, jnp.where(own, vs, jnp.ones_like(vs)), preferred_element_type=F32))
        o_ref[:, lsl] = (_pair_normalise(res, low) * za_ref[:, lsl].astype(F32)).astype(BF16)


NA_ROWS = 4
NA_KROWS = WIN_R + NA_ROWS - 1
NA_Q = NA_ROWS * GRID_W
NA_K = NA_KROWS * GRID_W


def _na_union_start(i, rows):
    return jnp.clip(NA_ROWS * i - WIN_R // 2, 0, rows - NA_KROWS)


def _na_cases(rows):
    cases, step_case = [], []
    for i in range(rows // NA_ROWS):
        u0 = min(max(NA_ROWS * i - WIN_R // 2, 0), rows - NA_KROWS)
        los = tuple(min(max(NA_ROWS * i + j - WIN_R // 2, 0), rows - WIN_R) - u0 for j in range(NA_ROWS))
        case = (u0 - NA_ROWS * i + WIN_R - 1, los)
        if case not in cases:
            cases.append(case)
        step_case.append(cases.index(case))
    return cases, step_case


def _select_case(idx, values):
    out = values[-1]
    for t in range(len(values) - 2, -1, -1):
        out = jnp.where(idx == t, values[t], out)
    return out


def _bias_kernel(src_ref, o_ref, *, rows):
    ty = pl.program_id(1)
    cases, _ = _na_cases(rows)
    c = _select_case(ty, [cs_[0] for cs_ in cases])
    qc = jax.lax.broadcasted_iota(jnp.int32, (GRID_W, 128), 0)
    lane = jax.lax.broadcasted_iota(jnp.int32, (GRID_W, 128), 1)
    upper = lane >= GRID_W
    kc = jnp.where(upper, lane - GRID_W, lane)
    cs = jnp.clip(qc - WIN_C // 2, 0, GRID_W - WIN_C)
    col_ok = (kc >= cs) & (kc < cs + WIN_C)
    for hh in range(N_HEADS):
        for j in range(NA_ROWS):
            lo = _select_case(ty, [cs_[1][j] for cs_ in cases])
            for p in range((NA_KROWS + 1) // 2):
                sidx = jnp.clip(2 * p - j + c + 1, 0, 15)
                row = src_ref[0, hh, pl.ds(sidx, 1), :]
                t = pltpu.roll(jnp.broadcast_to(row, (GRID_W, 128)), 0, 1, stride=1, stride_axis=0)
                kk = jnp.where(upper, 2 * p + 1, 2 * p)
                ok = col_ok & (kk >= lo) & (kk < lo + WIN_R)
                val = jnp.where(ok, t * LOG2E, NEG_INF)
                rsl = slice(j * GRID_W, (j + 1) * GRID_W)
                if 2 * p + 1 < NA_KROWS:
                    o_ref[0, 0, hh, rsl, p * 128:(p + 1) * 128] = val
                else:
                    o_ref[0, 0, hh, rsl, p * 128:p * 128 + GRID_W] = val[:, :GRID_W]


def _bias_table(rpb, rows):
    nr = 2 * WIN_R - 1
    pad = jnp.pad(rpb, ((0, 0), (0, 0), (1, 1), (0, 0)))
    lo_half, hi_half = pad[:, :, 0:nr + 1], pad[:, :, 1:nr + 2]
    src = jnp.zeros((DEPTH, N_HEADS, nr + 1, 128), F32)
    src = src.at[..., 0:WIN_C].set(lo_half[..., WIN_C - 1:])
    src = src.at[..., 128 - (WIN_C - 1):].set(lo_half[..., :WIN_C - 1])
    src = src.at[..., GRID_W - (WIN_C - 1):GRID_W + WIN_C].set(hi_half)
    ncase = len(_na_cases(rows)[0])
    return pl.pallas_call(
        functools.partial(_bias_kernel, rows=rows),
        grid=(DEPTH, ncase),
        in_specs=[pl.BlockSpec((1, N_HEADS, nr + 1, 128), lambda l, ty: (l, 0, 0, 0))],
        out_specs=pl.BlockSpec((1, 1, N_HEADS, NA_Q, NA_K), lambda l, ty: (l, ty, 0, 0, 0)),
        out_shape=jax.ShapeDtypeStruct((DEPTH, ncase, N_HEADS, NA_Q, NA_K), F32),
        compiler_params=_params(2),
        name="rpb_table",
    )(src)


def _na_attn_kernel(q_ref, k_ref, v_ref, kc_ref, vc_ref, bias_ref, za_ref, o_ref,
                    vaug_scr, kcp_scr, vcaug_scr, *, rows):
    i = pl.program_id(1)
    npair = N_HEADS // 2
    past = kc_ref.shape[3]
    case = _select_case(i, _na_cases(rows)[1])

    @pl.when(i == 0)
    def _prepare():
        low = jax.lax.broadcasted_iota(jnp.int32, (v_ref.shape[0], 128), 1) < HEAD_DIM
        ones_c = jnp.ones((past, HEAD_DIM), F32)
        for g in range(npair):
            vs = v_ref[:, g * 128:(g + 1) * 128]
            vaug_scr[0, g] = jnp.where(low, vs, jnp.ones_like(vs))
            vaug_scr[1, g] = jnp.where(low, jnp.ones_like(vs), vs)
            kcp_scr[g] = jnp.concatenate([kc_ref[0, 0, 2 * g], kc_ref[0, 0, 2 * g + 1]], axis=-1).astype(BF16)
            vcaug_scr[0, g] = jnp.concatenate([vc_ref[0, 0, 2 * g], ones_c], axis=-1).astype(BF16)
            vcaug_scr[1, g] = jnp.concatenate([ones_c, vc_ref[0, 0, 2 * g + 1]], axis=-1).astype(BF16)

    start = pl.multiple_of(_na_union_start(i, rows) * GRID_W, GRID_W)
    lowq = jax.lax.broadcasted_iota(jnp.int32, (NA_Q, 128), 1) < HEAD_DIM
    for g in range(npair):
        lsl = slice(g * 128, (g + 1) * 128)
        qs = q_ref[:, lsl]
        kw = k_ref[pl.ds(start, NA_K), lsl]
        kcx = kcp_scr[g]
        res = []
        for hh in range(2):
            qz = jnp.where(lowq if hh == 0 else jnp.logical_not(lowq), qs, jnp.zeros_like(qs))
            s_w = _nt_dot(qz, kw) + bias_ref[0, case, 2 * g + hh]
            s_c = _nt_dot(qz, kcx)
            m = jnp.maximum(jnp.max(s_w, axis=-1, keepdims=True), jnp.max(s_c, axis=-1, keepdims=True))
            p_w = jnp.exp2(s_w - m).astype(BF16)
            p_c = jnp.exp2(s_c - m).astype(BF16)
            res.append(jnp.dot(p_w, vaug_scr[hh, g, pl.ds(start, NA_K), :], preferred_element_type=F32)
                       + jnp.dot(p_c, vcaug_scr[hh, g], preferred_element_type=F32))
        o_ref[:, lsl] = (_pair_normalise(res, lowq) * za_ref[:, lsl].astype(F32)).astype(BF16)


def _na_attn(q, k, v, cache_k, cache_v, layer, bias, za, seq):
    t = q.shape[0]
    nb = t // seq
    rows = seq // GRID_W
    assert rows % NA_ROWS == 0 and rows >= NA_KROWS
    steps = rows // NA_ROWS
    ncase = bias.shape[1]
    past = cache_k.shape[3]
    qblk = pl.BlockSpec((NA_Q, WIDTH_A), lambda b, i: (b * steps + i, 0))
    kvblk = pl.BlockSpec((seq, WIDTH_A), lambda b, i: (b, 0))
    cblk = pl.BlockSpec((1, 1, N_HEADS, past, HEAD_DIM), lambda b, i: (b, layer, 0, 0, 0))
    bblk = pl.BlockSpec((1, ncase, N_HEADS, NA_Q, NA_K), lambda b, i: (layer, 0, 0, 0, 0),
                        pipeline_mode=pl.Buffered(1))
    return pl.pallas_call(
        functools.partial(_na_attn_kernel, rows=rows),
        grid=(nb, steps),
        in_specs=[qblk, kvblk, kvblk, cblk, cblk, bblk, qblk],
        out_specs=qblk,
        out_shape=jax.ShapeDtypeStruct((t, WIDTH_A), BF16),
        scratch_shapes=[
            pltpu.VMEM((2, N_HEADS // 2, seq, 128), BF16),
            pltpu.VMEM((N_HEADS // 2, past, 128), BF16),
            pltpu.VMEM((2, N_HEADS // 2, past, 128), BF16),
        ],
        compiler_params=_params(2),
        name="na_attn",
    )(q, k, v, cache_k, cache_v, bias, za)


def _fourier_kernel(c_ref, s_ref, ucs_ref, wf_ref, zb_ref, o_ref):
    uc = ucs_ref[:, 0:WIDTH_B]
    us = ucs_ref[:, WIDTH_B:2 * WIDTH_B]
    y = (jnp.dot(c_ref[...], uc, preferred_element_type=F32)
         - jnp.dot(s_ref[...], us, preferred_element_type=F32))
    y2 = jnp.dot(y.astype(BF16), wf_ref[0], preferred_element_type=F32)
    o_ref[...] = (y2 * zb_ref[...].astype(F32)).astype(BF16)


def _fourier(ucs, zb, cmat, smat, wf, layer, seq, tk):
    t = ucs.shape[0]
    nb = t // seq
    nk = seq // tk
    return pl.pallas_call(
        _fourier_kernel,
        grid=(nk, nb),
        in_specs=[
            pl.BlockSpec((tk, seq), lambda k, b: (k, 0)),
            pl.BlockSpec((tk, seq), lambda k, b: (k, 0)),
            pl.BlockSpec((seq, 2 * WIDTH_B), lambda k, b: (b, 0)),
            _layer_spec((WIDTH_B, WIDTH_B), layer),
            pl.BlockSpec((tk, WIDTH_B), lambda k, b: (b * nk + k, 0)),
        ],
        out_specs=pl.BlockSpec((tk, WIDTH_B), lambda k, b: (b * nk + k, 0)),
        out_shape=jax.ShapeDtypeStruct((t, WIDTH_B), BF16),
        compiler_params=_params(2),
        name="fourier",
    )(cmat, smat, ucs, wf, zb)


def _dft_consts(seq):
    n = np.arange(seq, dtype=np.int64)
    ang = 2.0 * np.pi * ((n[:, None] * n[None, :]) % seq).astype(np.float64) / seq
    nrm = 1.0 / np.sqrt(float(seq) * GROUP_B)
    cmat = (np.cos(ang) * nrm).astype(np.float32)
    smat = (np.sin(ang) * nrm).astype(np.float32)
    return cmat, smat


def _group_dft_const():
    m = np.arange(GROUP_B, dtype=np.int64)
    ang = 2.0 * np.pi * ((m[:, None] * m[None, :]) % GROUP_B).astype(np.float64) / GROUP_B
    eye = np.eye(N_GROUPS_B)
    return np.concatenate([np.kron(eye, np.cos(ang)), np.kron(eye, np.sin(ang))], axis=1).astype(np.float32)


def _pool_kernel(u_ref, zc_ref, w_ref, ps_ref, o_ref, pad_ref, *, seq):
    halo = POOL_HALO
    pad_ref[0:halo, :] = jnp.zeros((halo, WIDTH_C), F32)
    pad_ref[halo + seq:2 * halo + seq, :] = jnp.zeros((halo, WIDTH_C), F32)
    pad_ref[halo:halo + seq, :] = u_ref[...]
    ext = POOL_TILE + 2 * halo
    lane = jax.lax.broadcasted_iota(jnp.int32, (POOL_TILE, WIDTH_C), 1)
    g0, g1, g2 = lane < GROUP_C, lane < 2 * GROUP_C, lane < 3 * GROUP_C
    half = jnp.where(g0, POOL_WINDOWS[0] // 2,
                     jnp.where(g1, POOL_WINDOWS[1] // 2,
                               jnp.where(g2, POOL_WINDOWS[2] // 2, POOL_WINDOWS[3] // 2)))
    rowi = jax.lax.broadcasted_iota(jnp.int32, (POOL_TILE, WIDTH_C), 0)
    for ti in range(seq // POOL_TILE):
        t0 = ti * POOL_TILE
        x = pad_ref[t0:t0 + ext, :]
        a1 = x + pltpu.roll(x, 1, 0)
        w4 = pltpu.roll(a1, 1, 0) + pltpu.roll(a1, ext - 1, 0)
        w8 = pltpu.roll(w4, 2, 0) + pltpu.roll(w4, ext - 2, 0)
        w16 = pltpu.roll(w8, 4, 0) + pltpu.roll(w8, ext - 4, 0)
        mid = slice(halo, halo + POOL_TILE)
        wsum = jnp.where(g0, a1[mid], jnp.where(g1, w4[mid], jnp.where(g2, w8[mid], w16[mid])))
        tpos = rowi + t0
        cnt = (jnp.minimum(tpos + half, seq) - jnp.maximum(tpos - half, 0)).astype(F32)
        dlt = wsum / cnt - x[mid]
        y = jnp.dot(dlt.astype(BF16), w_ref[...], preferred_element_type=F32) * ps_ref[...]
        o_ref[t0:t0 + POOL_TILE, :] = (y * zc_ref[t0:t0 + POOL_TILE, :].astype(F32)).astype(BF16)


def _pool(uc, zc, wp, ps, seq):
    t = uc.shape[0]
    blk = pl.BlockSpec((seq, WIDTH_C), lambda b: (b, 0))
    return pl.pallas_call(
        functools.partial(_pool_kernel, seq=seq),
        grid=(t // seq,),
        in_specs=[blk, blk, _const_spec((WIDTH_C, WIDTH_C)), _const_spec((1, WIDTH_C))],
        out_specs=blk,
        out_shape=jax.ShapeDtypeStruct((t, WIDTH_C), BF16),
        scratch_shapes=[pltpu.VMEM((seq + 2 * POOL_HALO, WIDTH_C), F32)],
        compiler_params=_params(1),
        name="pool",
    )(uc, zc, wp, ps)


def _ctx_branches_kernel(q_ref, k_ref, v_ref, za_ref, ucs_ref, zb_ref, uc_ref, zc_ref,
                         c_ref, s_ref, wf_ref, wp_ref, ps_ref, ya_o, yb_o, yc_o, pad_ref, *, seq):
    _ctx_attention(q_ref, k_ref, v_ref, za_ref, ya_o)
    _fourier_kernel(c_ref, s_ref, ucs_ref, wf_ref, zb_ref, yb_o)
    _pool_kernel(uc_ref, zc_ref, wp_ref, ps_ref, yc_o, pad_ref, seq=seq)


def _ctx_branches(q, k, v, za, ucs, zb, uc, zc, cmat, smat, wf, layer, wp, ps, seq):
    t = q.shape[0]
    blk = lambda w: pl.BlockSpec((seq, w), lambda b: (b, 0))
    return pl.pallas_call(
        functools.partial(_ctx_branches_kernel, seq=seq),
        grid=(t // seq,),
        in_specs=[blk(WIDTH_A)] * 4 + [blk(2 * WIDTH_B), blk(WIDTH_B), blk(WIDTH_C), blk(WIDTH_C),
                                       _const_spec((seq, seq)), _const_spec((seq, seq)),
                                       _layer_spec((WIDTH_B, WIDTH_B), layer),
                                       _const_spec((WIDTH_C, WIDTH_C)), _const_spec((1, WIDTH_C))],
        out_specs=[blk(WIDTH_A), blk(WIDTH_B), blk(WIDTH_C)],
        out_shape=[jax.ShapeDtypeStruct((t, w), BF16) for w in (WIDTH_A, WIDTH_B, WIDTH_C)],
        scratch_shapes=[pltpu.VMEM((seq + 2 * POOL_HALO, WIDTH_C), F32)],
        compiler_params=_params(1),
        name="ctx_branches",
    )(q, k, v, za, ucs, zb, uc, zc, cmat, smat, wf, wp, ps)


def _merge_kernel(x_ref, mod_ref, ng_ref, wg_ref, ya_ref, yb_ref, yc_ref,
                  pa_ref, pb_ref, pc_ref, wo_ref, o_ref):
    x = x_ref[...]
    hb = _modulated_norm(x, mod_ref, ng_ref)
    gate = mod_ref[0, 2:3, :]

    def gated(y_ref, p_ref, col):
        g = jnp.dot(hb, wg_ref[0, :, col * D_MODEL:(col + 1) * D_MODEL], preferred_element_type=F32)
        return _sigmoid(g) * jnp.dot(y_ref[...], p_ref[0], preferred_element_type=F32)

    m = gated(ya_ref, pa_ref, 0) + gated(yb_ref, pb_ref, 1) + gated(yc_ref, pc_ref, 2)
    o_ref[...] = x + gate * jnp.dot(m.astype(BF16), wo_ref[0], preferred_element_type=F32)


def _merge(x2, mod, mod_seq, layer, ng, w_in, ya, yb, yc, pa, pb, pc, wo, tm=ROW_TILE):
    t = x2.shape[0]
    tiles_per_seq = mod_seq // tm
    row = lambda w: pl.BlockSpec((tm, w), lambda i: (i, 0))
    return pl.pallas_call(
        _merge_kernel,
        grid=(t // tm,),
        in_specs=[
            row(D_MODEL),
            pl.BlockSpec((1, 3, D_MODEL), lambda i: (i // tiles_per_seq, 0, 0)),
            _const_spec((1, D_MODEL)),
            pl.BlockSpec((1, D_MODEL, 3 * D_MODEL), lambda i: (layer, 0, 1)),
            row(WIDTH_A), row(WIDTH_B), row(WIDTH_C),
            _layer_spec((WIDTH_A, D_MODEL), layer), _layer_spec((WIDTH_B, D_MODEL), layer),
            _layer_spec((WIDTH_C, D_MODEL), layer), _layer_spec((D_MODEL, D_MODEL), layer),
        ],
        out_specs=row(D_MODEL),
        out_shape=jax.ShapeDtypeStruct((t, D_MODEL), F32),
        compiler_params=_params(1),
        name="merge",
    )(x2, mod, ng, w_in, ya, yb, yc, pa, pb, pc, wo)


def _block_diag(w):
    g, c, e = w.shape
    eye = jnp.eye(g, dtype=w.dtype)
    return (eye[:, None, :, None] * w[:, :, None, :]).reshape(g * c, g * e)


def kernel(x_prompt, x_sample, cache_k, cache_v, c, c_ctx, norm_g, w_ada, b_ada, w_in, q_norm_g, k_norm_g,
           rpb, w_fnet, w_pool, pool_scale, p_a, p_b, p_c, w_o):
    nb_p, seq_p, _ = x_prompt.shape
    nb_s, seq_s, _ = x_sample.shape

    cond = jnp.concatenate([c_ctx[None, :], c, jnp.zeros((8 - 1 - nb_s, D_MODEL), F32)], axis=0)
    ada = _ada(cond, w_ada, b_ada).reshape(DEPTH, 8, 3, D_MODEL)
    bias = _bias_table(rpb, seq_s // GRID_W)

    head_mean = jnp.asarray(np.kron(np.eye(N_HEADS), np.full((HEAD_DIM, HEAD_DIM), 1.0 / HEAD_DIM)), BF16)
    bd = jnp.asarray(_group_dft_const()).astype(BF16)
    dft_p = [jnp.asarray(m).astype(BF16) for m in _dft_consts(seq_p)]
    dft_s = [jnp.asarray(m).astype(BF16) for m in _dft_consts(seq_s)]

    w_in_b, wf_b = w_in.astype(BF16), w_fnet.astype(BF16)
    pa, pb, pc, wo = p_a.astype(BF16), p_b.astype(BF16), p_c.astype(BF16), w_o.astype(BF16)

    xp = x_prompt.reshape(nb_p * seq_p, D_MODEL)
    xs = x_sample.reshape(nb_s * seq_s, D_MODEL)
    new_kv = None
    for l in range(DEPTH):
        ng = norm_g[l][None, :]
        qg = jnp.tile(q_norm_g[l], N_HEADS)[None, :]
        kg = jnp.tile(k_norm_g[l], N_HEADS)[None, :]
        wp = _block_diag(w_pool[l]).astype(BF16)
        ps = pool_scale[l][None, :]
        mod_p = ada[l, 0:1]
        mod_s = ada[l, 1:1 + nb_s]

        (q, k, v, za, ucs, zb, uc, zc, kf, vf) = _inproj(
            xp, mod_p, seq_p, nb_p * seq_p, l, ng, w_in_b, qg, kg, head_mean, bd, prev_kv=new_kv, kv_cache=True)
        new_kv = (kf, vf)
        ya, yb, yc = _ctx_branches(q, k, v, za, ucs, zb, uc, zc, dft_p[0], dft_p[1], wf_b, l, wp, ps, seq_p)
        xp = _merge(xp, mod_p, nb_p * seq_p, l, ng, w_in_b, ya, yb, yc, pa, pb, pc, wo)

        (q, k, v, za, ucs, zb, uc, zc) = _inproj(
            xs, mod_s, seq_s, seq_s, l, ng, w_in_b, qg, kg, head_mean, bd)
        ya = _na_attn(q, k, v, cache_k, cache_v, l, bias, za, seq_s)
        yb = _fourier(ucs, zb, dft_s[0], dft_s[1], wf_b, l, seq_s, tk=512)
        yc = _pool(uc, zc, wp, ps, seq_s)
        xs = _merge(xs, mod_s, seq_s, l, ng, w_in_b, ya, yb, yc, pa, pb, pc, wo)

    return (xp.reshape(nb_p, seq_p, D_MODEL), xs.reshape(nb_s, seq_s, D_MODEL), new_kv[0], new_kv[1])
```

```python
import functools

import numpy as np
import jax
import jax.numpy as jnp
from jax.experimental import pallas as pl
from jax.experimental.pallas import tpu as pltpu

D_MODEL = 1024
DEPTH = 2
GRID_W = 64
HEAD_DIM = 64
WIDTH_A = 512
N_HEADS = 8
WIN_R = 8
WIN_C = 16
WIDTH_B = 256
N_GROUPS_B = 4
GROUP_B = 64
WIDTH_C = 256
POOL_WINDOWS = (2, 4, 8, 16)
GROUP_C = 64
EPS = 1e-6
NEG_INF = -1e30
LOG2E = 1.4426950408889634
Q_SCALE = HEAD_DIM ** -0.5 * LOG2E
BASE_W = 4 * WIDTH_A + 2 * WIDTH_B + 2 * WIDTH_C
IN_WIDTH = BASE_W + 3 * D_MODEL

OFF_Q, OFF_K, OFF_V, OFF_ZA = 0, 512, 1024, 1536
OFF_UB, OFF_ZB, OFF_UC, OFF_ZC = 2048, 2304, 2560, 2816
assert BASE_W == 3 * D_MODEL

POOL_HALO = 16
POOL_TILE = 128
ROW_TILE = 512
VMEM_LIMIT = 56 * 1024 * 1024

BF16 = jnp.bfloat16
F32 = jnp.float32


def _params(n_axes, vmem=VMEM_LIMIT):
    return pltpu.CompilerParams(dimension_semantics=("arbitrary",) * n_axes, vmem_limit_bytes=vmem)


def _const_spec(shape):
    zeros = (0,) * len(shape)
    return pl.BlockSpec(shape, lambda *_: zeros)


def _sigmoid(x):
    return 1.0 / (1.0 + jnp.exp(-x))


def _ada_kernel(cond_ref, w_ref, b_ref, o_ref):
    cnd = cond_ref[...]
    a = cnd * _sigmoid(cnd)
    o_ref[0] = jnp.dot(a, w_ref[0], preferred_element_type=F32) + b_ref[0]


def _ada(cond, w_ada, b_ada):
    tn = 1024
    return pl.pallas_call(
        _ada_kernel,
        grid=(DEPTH, 3 * D_MODEL // tn),
        in_specs=[
            pl.BlockSpec((8, D_MODEL), lambda l, j: (0, 0)),
            pl.BlockSpec((1, D_MODEL, tn), lambda l, j: (l, 0, j)),
            pl.BlockSpec((1, 1, tn), lambda l, j: (l, 0, j)),
        ],
        out_specs=pl.BlockSpec((1, 8, tn), lambda l, j: (l, 0, j)),
        out_shape=jax.ShapeDtypeStruct((DEPTH, 8, 3 * D_MODEL), F32),
        compiler_params=_params(2),
        name="ada",
    )(cond, w_ada, b_ada.reshape(DEPTH, 1, 3 * D_MODEL))


def _modulated_norm(x, mod_ref, ng_ref):
    shift = mod_ref[0, 0:1, :]
    scale = mod_ref[0, 1:2, :]
    y = x * jax.lax.rsqrt(jnp.mean(x * x, axis=-1, keepdims=True) + EPS)
    return ((y * ng_ref[...]) * (1.0 + scale) + shift).astype(BF16)


def _inproj_kernel(x_ref, mod_ref, ng_ref, w_ref, qg_ref, kg_ref, hm_ref, bd_ref, *rest, layer, kv_cache):
    n_prev = 2 if (kv_cache and layer > 0) else 0
    prev, outs = rest[:n_prev], rest[n_prev:]
    if kv_cache:
        (q_o, k_o, v_o, za_o, ucs_o, zb_o, uc_o, zc_o, kf_o, vf_o) = outs
    else:
        (q_o, k_o, v_o, za_o, ucs_o, zb_o, uc_o, zc_o) = outs
    hb = _modulated_norm(x_ref[...], mod_ref, ng_ref)

    def proj(off, width):
        return jnp.dot(hb, w_ref[0, :, off:off + width], preferred_element_type=F32)

    def head_norm(t, g_ref):
        ms = jnp.dot((t * t).astype(BF16), hm_ref[...], preferred_element_type=F32)
        return t * jax.lax.rsqrt(ms + EPS) * g_ref[...]

    pq = proj(OFF_Q, WIDTH_A)
    pk = proj(OFF_K, WIDTH_A)
    q = head_norm(pq, qg_ref)
    q_o[...] = (q * Q_SCALE).astype(BF16)
    v = proj(OFF_V, WIDTH_A)
    k = head_norm(pk, kg_ref)
    k_o[...] = k.astype(BF16)
    za = proj(OFF_ZA, WIDTH_A)
    v_o[...] = v.astype(BF16)
    if kv_cache:
        seq = kf_o.shape[3]
        if layer > 0:
            kf_o[:, 0:layer] = prev[0][...]
            vf_o[:, 0:layer] = prev[1][...]
        for si in range(kf_o.shape[0]):
            for hh in range(N_HEADS):
                rsl, csl = slice(si * seq, (si + 1) * seq), slice(hh * HEAD_DIM, (hh + 1) * HEAD_DIM)
                kf_o[si, layer, hh] = k[rsl, csl]
                vf_o[si, layer, hh] = v[rsl, csl]
    ub = proj(OFF_UB, WIDTH_B)
    za_o[...] = (za * _sigmoid(za)).astype(BF16)
    zb = proj(OFF_ZB, WIDTH_B)
    ucs_o[...] = jnp.dot(ub.astype(BF16), bd_ref[...], preferred_element_type=F32).astype(BF16)
    uc = proj(OFF_UC, WIDTH_C)
    zb_o[...] = (zb * _sigmoid(zb)).astype(BF16)
    zc = proj(OFF_ZC, WIDTH_C)
    uc_o[...] = uc
    zc_o[...] = (zc * _sigmoid(zc)).astype(BF16)


def _layer_spec(shape, layer):
    zeros = (0,) * len(shape)
    return pl.BlockSpec((1,) + tuple(shape), lambda *_: (layer,) + zeros)


def _inproj(x2, mod, seq, mod_seq, layer, ng, w_in, qg, kg, hm, bd, prev_kv=None, kv_cache=False, tm=ROW_TILE):
    t = x2.shape[0]
    tiles_per_mod = mod_seq // tm
    row = lambda w: pl.BlockSpec((tm, w), lambda i: (i, 0))
    widths = [(WIDTH_A, BF16)] * 4 + [(2 * WIDTH_B, BF16), (WIDTH_B, BF16), (WIDTH_C, F32), (WIDTH_C, BF16)]
    in_specs = [
        row(D_MODEL),
        pl.BlockSpec((1, 3, D_MODEL), lambda i: (i // tiles_per_mod, 0, 0)),
        _const_spec((1, D_MODEL)),
        pl.BlockSpec((1, D_MODEL, BASE_W), lambda i: (layer, 0, 0)),
        _const_spec((1, WIDTH_A)),
        _const_spec((1, WIDTH_A)),
        _const_spec((WIDTH_A, WIDTH_A)),
        _const_spec((WIDTH_B, 2 * WIDTH_B)),
    ]
    args = [x2, mod, ng, w_in, qg, kg, hm, bd]
    out_specs = [row(w) for w, _ in widths]
    out_shape = [jax.ShapeDtypeStruct((t, w), dt) for w, dt in widths]
    if kv_cache:
        assert tm % seq == 0
        cache = lambda n: pl.BlockSpec((tm // seq, n, N_HEADS, seq, HEAD_DIM), lambda i: (i, 0, 0, 0, 0))
        if layer > 0:
            in_specs += [cache(layer)] * 2
            args += list(prev_kv)
        out_specs += [cache(layer + 1)] * 2
        out_shape += [jax.ShapeDtypeStruct((t // seq, layer + 1, N_HEADS, seq, HEAD_DIM), F32)] * 2
    return pl.pallas_call(
        functools.partial(_inproj_kernel, layer=layer, kv_cache=kv_cache),
        grid=(t // tm,),
        in_specs=in_specs,
        out_specs=out_specs,
        out_shape=out_shape,
        compiler_params=_params(1),
        name="inproj",
    )(*args)


def _nt_dot(a, b):
    return jax.lax.dot_general(a, b, (((1,), (1,)), ((), ())), preferred_element_type=F32)


def _pair_normalise(res, low):
    num = jnp.where(low, res[0], res[1])
    den = pltpu.roll(jnp.where(low, res[1], res[0]), HEAD_DIM, 1)
    return num / den


def _ctx_attention(q_ref, k_ref, v_ref, za_ref, o_ref):
    low = jax.lax.broadcasted_iota(jnp.int32, (q_ref.shape[0], 128), 1) < HEAD_DIM

    def own(h):
        return low if h % 2 == 0 else jnp.logical_not(low)

    def scores(h):
        lsl = slice((h // 2) * 128, (h // 2 + 1) * 128)
        qs = q_ref[:, lsl]
        return _nt_dot(jnp.where(own(h), qs, jnp.zeros_like(qs)), k_ref[:, lsl])

    sc, res = {}, {}
    for t in range(N_HEADS + 1):
        if t < N_HEADS:
            sc[t] = scores(t)
        if t >= 1:
            h = t - 1
            lsl = slice((h // 2) * 128, (h // 2 + 1) * 128)
            s = sc.pop(h)
            p = jnp.exp2(s - jnp.max(s, axis=-1, keepdims=True)).astype(BF16)
            vs = v_ref[:, lsl]
            res[h] = jnp.dot(p, jnp.where(own(h), vs, jnp.ones_like(vs)), preferred_element_type=F32)
            if h % 2 == 1:
                pair = [res.pop(h - 1), res.pop(h)]
                o_ref[:, lsl] = (_pair_normalise(pair, low) * za_ref[:, lsl].astype(F32)).astype(BF16)


NA_ROWS = 4
NA_KROWS = WIN_R + NA_ROWS - 1
NA_Q = NA_ROWS * GRID_W
NA_K = NA_KROWS * GRID_W


def _na_union_start(i, rows):
    return jnp.clip(NA_ROWS * i - WIN_R // 2, 0, rows - NA_KROWS)


def _na_cases(rows):
    cases, step_case = [], []
    for i in range(rows // NA_ROWS):
        u0 = min(max(NA_ROWS * i - WIN_R // 2, 0), rows - NA_KROWS)
        los = tuple(min(max(NA_ROWS * i + j - WIN_R // 2, 0), rows - WIN_R) - u0 for j in range(NA_ROWS))
        case = (u0 - NA_ROWS * i + WIN_R - 1, los)
        if case not in cases:
            cases.append(case)
        step_case.append(cases.index(case))
    return cases, step_case


def _select_case(idx, values):
    out = values[-1]
    for t in range(len(values) - 2, -1, -1):
        out = jnp.where(idx == t, values[t], out)
    return out


def _bias_kernel(src_ref, o_ref, *, rows):
    ty = pl.program_id(1)
    cases, _ = _na_cases(rows)
    c = _select_case(ty, [cs_[0] for cs_ in cases])
    qc = jax.lax.broadcasted_iota(jnp.int32, (GRID_W, 128), 0)
    lane = jax.lax.broadcasted_iota(jnp.int32, (GRID_W, 128), 1)
    upper = lane >= GRID_W
    kc = jnp.where(upper, lane - GRID_W, lane)
    cs = jnp.clip(qc - WIN_C // 2, 0, GRID_W - WIN_C)
    col_ok = (kc >= cs) & (kc < cs + WIN_C)
    for hh in range(N_HEADS):
        for j in range(NA_ROWS):
            lo = _select_case(ty, [cs_[1][j] for cs_ in cases])
            for p in range((NA_KROWS + 1) // 2):
                sidx = jnp.clip(2 * p - j + c + 1, 0, 15)
                row = src_ref[0, hh, pl.ds(sidx, 1), :]
                t = pltpu.roll(jnp.broadcast_to(row, (GRID_W, 128)), 0, 1, stride=1, stride_axis=0)
                kk = jnp.where(upper, 2 * p + 1, 2 * p)
                ok = col_ok & (kk >= lo) & (kk < lo + WIN_R)
                val = jnp.where(ok, t * LOG2E, NEG_INF)
                rsl = slice(j * GRID_W, (j + 1) * GRID_W)
                if 2 * p + 1 < NA_KROWS:
                    o_ref[0, 0, hh, rsl, p * 128:(p + 1) * 128] = val
                else:
                    o_ref[0, 0, hh, rsl, p * 128:p * 128 + GRID_W] = val[:, :GRID_W]


def _bias_table(rpb, rows):
    nr = 2 * WIN_R - 1
    pad = jnp.pad(rpb, ((0, 0), (0, 0), (1, 1), (0, 0)))
    lo_half, hi_half = pad[:, :, 0:nr + 1], pad[:, :, 1:nr + 2]
    src = jnp.zeros((DEPTH, N_HEADS, nr + 1, 128), F32)
    src = src.at[..., 0:WIN_C].set(lo_half[..., WIN_C - 1:])
    src = src.at[..., 128 - (WIN_C - 1):].set(lo_half[..., :WIN_C - 1])
    src = src.at[..., GRID_W - (WIN_C - 1):GRID_W + WIN_C].set(hi_half)
    ncase = len(_na_cases(rows)[0])
    return pl.pallas_call(
        functools.partial(_bias_kernel, rows=rows),
        grid=(DEPTH, ncase),
        in_specs=[pl.BlockSpec((1, N_HEADS, nr + 1, 128), lambda l, ty: (l, 0, 0, 0))],
        out_specs=pl.BlockSpec((1, 1, N_HEADS, NA_Q, NA_K), lambda l, ty: (l, ty, 0, 0, 0)),
        out_shape=jax.ShapeDtypeStruct((DEPTH, ncase, N_HEADS, NA_Q, NA_K), F32),
        compiler_params=_params(2),
        name="rpb_table",
    )(src)


def _na_attn_kernel(q_ref, k_ref, v_ref, kc_ref, vc_ref, bias_ref, za_ref, o_ref,
                    vaug_scr, kcp_scr, vcaug_scr, *, rows):
    i = pl.program_id(1)
    npair = N_HEADS // 2
    past = kc_ref.shape[3]
    case = _select_case(i, _na_cases(rows)[1])

    @pl.when(i == 0)
    def _prepare():
        low = jax.lax.broadcasted_iota(jnp.int32, (v_ref.shape[0], 128), 1) < HEAD_DIM
        ones_c = jnp.ones((past, HEAD_DIM), F32)
        for g in range(npair):
            vs = v_ref[:, g * 128:(g + 1) * 128]
            vaug_scr[0, g] = jnp.where(low, vs, jnp.ones_like(vs))
            vaug_scr[1, g] = jnp.where(low, jnp.ones_like(vs), vs)
            kcp_scr[g] = jnp.concatenate([kc_ref[0, 0, 2 * g], kc_ref[0, 0, 2 * g + 1]], axis=-1).astype(BF16)
            vcaug_scr[0, g] = jnp.concatenate([vc_ref[0, 0, 2 * g], ones_c], axis=-1).astype(BF16)
            vcaug_scr[1, g] = jnp.concatenate([ones_c, vc_ref[0, 0, 2 * g + 1]], axis=-1).astype(BF16)

    start = pl.multiple_of(_na_union_start(i, rows) * GRID_W, GRID_W)
    lowq = jax.lax.broadcasted_iota(jnp.int32, (NA_Q, 128), 1) < HEAD_DIM

    def scores(h):
        g, hh = divmod(h, 2)
        lsl = slice(g * 128, (g + 1) * 128)
        qs = q_ref[:, lsl]
        qz = jnp.where(lowq if hh == 0 else jnp.logical_not(lowq), qs, jnp.zeros_like(qs))
        s_w = _nt_dot(qz, k_ref[pl.ds(start, NA_K), lsl]) + bias_ref[0, case, h]
        s_c = _nt_dot(qz, kcp_scr[g])
        m = jnp.maximum(jnp.max(s_w, axis=-1, keepdims=True), jnp.max(s_c, axis=-1, keepdims=True))
        return s_w, s_c, m

    def probs(sc):
        s_w, s_c, m = sc
        return jnp.exp2(s_w - m).astype(BF16), jnp.exp2(s_c - m).astype(BF16)

    def weighted(h, pr):
        g, hh = divmod(h, 2)
        return (jnp.dot(pr[0], vaug_scr[hh, g, pl.ds(start, NA_K), :], preferred_element_type=F32)
                + jnp.dot(pr[1], vcaug_scr[hh, g], preferred_element_type=F32))

    sc, res = {}, {}
    for t in range(N_HEADS + 1):
        if t < N_HEADS:
            sc[t] = scores(t)
        if t >= 1:
            h = t - 1
            res[h] = weighted(h, probs(sc.pop(h)))
            if h % 2 == 1:
                lsl = slice((h // 2) * 128, (h // 2 + 1) * 128)
                pair = [res.pop(h - 1), res.pop(h)]
                o_ref[:, lsl] = (_pair_normalise(pair, lowq) * za_ref[:, lsl].astype(F32)).astype(BF16)


def _na_attn(q, k, v, cache_k, cache_v, layer, bias, za, seq):
    t = q.shape[0]
    nb = t // seq
    rows = seq // GRID_W
    assert rows % NA_ROWS == 0 and rows >= NA_KROWS
    steps = rows // NA_ROWS
    ncase = bias.shape[1]
    past = cache_k.shape[3]
    qblk = pl.BlockSpec((NA_Q, WIDTH_A), lambda b, i: (b * steps + i, 0))
    kvblk = pl.BlockSpec((seq, WIDTH_A), lambda b, i: (b, 0))
    cblk = pl.BlockSpec((1, 1, N_HEADS, past, HEAD_DIM), lambda b, i: (b, layer, 0, 0, 0))
    bblk = pl.BlockSpec((1, ncase, N_HEADS, NA_Q, NA_K), lambda b, i: (layer, 0, 0, 0, 0),
                        pipeline_mode=pl.Buffered(1))
    return pl.pallas_call(
        functools.partial(_na_attn_kernel, rows=rows),
        grid=(nb, steps),
        in_specs=[qblk, kvblk, kvblk, cblk, cblk, bblk, qblk],
        out_specs=qblk,
        out_shape=jax.ShapeDtypeStruct((t, WIDTH_A), BF16),
        scratch_shapes=[
            pltpu.VMEM((2, N_HEADS // 2, seq, 128), BF16),
            pltpu.VMEM((N_HEADS // 2, past, 128), BF16),
            pltpu.VMEM((2, N_HEADS // 2, past, 128), BF16),
        ],
        compiler_params=_params(2),
        name="na_attn",
    )(q, k, v, cache_k, cache_v, bias, za)


def _fourier_kernel(c_ref, s_ref, ucs_ref, wf_ref, zb_ref, o_ref):
    uc = ucs_ref[:, 0:WIDTH_B]
    us = ucs_ref[:, WIDTH_B:2 * WIDTH_B]
    y = (jnp.dot(c_ref[...], uc, preferred_element_type=F32)
         - jnp.dot(s_ref[...], us, preferred_element_type=F32))
    y2 = jnp.dot(y.astype(BF16), wf_ref[0], preferred_element_type=F32)
    o_ref[...] = (y2 * zb_ref[...].astype(F32)).astype(BF16)


def _fourier(ucs, zb, cmat, smat, wf, layer, seq, tk):
    t = ucs.shape[0]
    nb = t // seq
    nk = seq // tk
    return pl.pallas_call(
        _fourier_kernel,
        grid=(nk, nb),
        in_specs=[
            pl.BlockSpec((tk, seq), lambda k, b: (k, 0)),
            pl.BlockSpec((tk, seq), lambda k, b: (k, 0)),
            pl.BlockSpec((seq, 2 * WIDTH_B), lambda k, b: (b, 0)),
            _layer_spec((WIDTH_B, WIDTH_B), layer),
            pl.BlockSpec((tk, WIDTH_B), lambda k, b: (b * nk + k, 0)),
        ],
        out_specs=pl.BlockSpec((tk, WIDTH_B), lambda k, b: (b * nk + k, 0)),
        out_shape=jax.ShapeDtypeStruct((t, WIDTH_B), BF16),
        compiler_params=_params(2),
        name="fourier",
    )(cmat, smat, ucs, wf, zb)


def _dft_consts(seq):
    n = np.arange(seq, dtype=np.int64)
    ang = 2.0 * np.pi * ((n[:, None] * n[None, :]) % seq).astype(np.float64) / seq
    nrm = 1.0 / np.sqrt(float(seq) * GROUP_B)
    cmat = (np.cos(ang) * nrm).astype(np.float32)
    smat = (np.sin(ang) * nrm).astype(np.float32)
    return cmat, smat


def _group_dft_const():
    m = np.arange(GROUP_B, dtype=np.int64)
    ang = 2.0 * np.pi * ((m[:, None] * m[None, :]) % GROUP_B).astype(np.float64) / GROUP_B
    eye = np.eye(N_GROUPS_B)
    return np.concatenate([np.kron(eye, np.cos(ang)), np.kron(eye, np.sin(ang))], axis=1).astype(np.float32)


def _pool_kernel(u_ref, zc_ref, w_ref, ps_ref, o_ref, pad_ref, *, seq):
    halo = POOL_HALO
    pad_ref[0:halo, :] = jnp.zeros((halo, WIDTH_C), F32)
    pad_ref[halo + seq:2 * halo + seq, :] = jnp.zeros((halo, WIDTH_C), F32)
    pad_ref[halo:halo + seq, :] = u_ref[...]
    ext = POOL_TILE + 2 * halo
    lane = jax.lax.broadcasted_iota(jnp.int32, (POOL_TILE, WIDTH_C), 1)
    g0, g1, g2 = lane < GROUP_C, lane < 2 * GROUP_C, lane < 3 * GROUP_C
    half = jnp.where(g0, POOL_WINDOWS[0] // 2,
                     jnp.where(g1, POOL_WINDOWS[1] // 2,
                               jnp.where(g2, POOL_WINDOWS[2] // 2, POOL_WINDOWS[3] // 2)))
    rowi = jax.lax.broadcasted_iota(jnp.int32, (POOL_TILE, WIDTH_C), 0)
    for ti in range(seq // POOL_TILE):
        t0 = ti * POOL_TILE
        x = pad_ref[t0:t0 + ext, :]
        a1 = x + pltpu.roll(x, 1, 0)
        w4 = pltpu.roll(a1, 1, 0) + pltpu.roll(a1, ext - 1, 0)
        w8 = pltpu.roll(w4, 2, 0) + pltpu.roll(w4, ext - 2, 0)
        w16 = pltpu.roll(w8, 4, 0) + pltpu.roll(w8, ext - 4, 0)
        mid = slice(halo, halo + POOL_TILE)
        wsum = jnp.where(g0, a1[mid], jnp.where(g1, w4[mid], jnp.where(g2, w8[mid], w16[mid])))
        tpos = rowi + t0
        cnt = (jnp.minimum(tpos + half, seq) - jnp.maximum(tpos - half, 0)).astype(F32)
        dlt = wsum / cnt - x[mid]
        y = jnp.dot(dlt.astype(BF16), w_ref[...], preferred_element_type=F32) * ps_ref[...]
        o_ref[t0:t0 + POOL_TILE, :] = (y * zc_ref[t0:t0 + POOL_TILE, :].astype(F32)).astype(BF16)


def _pool(uc, zc, wp, ps, seq):
    t = uc.shape[0]
    blk = pl.BlockSpec((seq, WIDTH_C), lambda b: (b, 0))
    return pl.pallas_call(
        functools.partial(_pool_kernel, seq=seq),
        grid=(t // seq,),
        in_specs=[blk, blk, _const_spec((WIDTH_C, WIDTH_C)), _const_spec((1, WIDTH_C))],
        out_specs=blk,
        out_shape=jax.ShapeDtypeStruct((t, WIDTH_C), BF16),
        scratch_shapes=[pltpu.VMEM((seq + 2 * POOL_HALO, WIDTH_C), F32)],
        compiler_params=_params(1),
        name="pool",
    )(uc, zc, wp, ps)


def _ctx_branches_kernel(q_ref, k_ref, v_ref, za_ref, ucs_ref, zb_ref, uc_ref, zc_ref,
                         c_ref, s_ref, wf_ref, wp_ref, ps_ref, ya_o, yb_o, yc_o, pad_ref, *, seq):
    _ctx_attention(q_ref, k_ref, v_ref, za_ref, ya_o)
    _fourier_kernel(c_ref, s_ref, ucs_ref, wf_ref, zb_ref, yb_o)
    _pool_kernel(uc_ref, zc_ref, wp_ref, ps_ref, yc_o, pad_ref, seq=seq)


def _ctx_branches(q, k, v, za, ucs, zb, uc, zc, cmat, smat, wf, layer, wp, ps, seq):
    t = q.shape[0]
    blk = lambda w: pl.BlockSpec((seq, w), lambda b: (b, 0))
    return pl.pallas_call(
        functools.partial(_ctx_branches_kernel, seq=seq),
        grid=(t // seq,),
        in_specs=[blk(WIDTH_A)] * 4 + [blk(2 * WIDTH_B), blk(WIDTH_B), blk(WIDTH_C), blk(WIDTH_C),
                                       _const_spec((seq, seq)), _const_spec((seq, seq)),
                                       _layer_spec((WIDTH_B, WIDTH_B), layer),
                                       _const_spec((WIDTH_C, WIDTH_C)), _const_spec((1, WIDTH_C))],
        out_specs=[blk(WIDTH_A), blk(WIDTH_B), blk(WIDTH_C)],
        out_shape=[jax.ShapeDtypeStruct((t, w), BF16) for w in (WIDTH_A, WIDTH_B, WIDTH_C)],
        scratch_shapes=[pltpu.VMEM((seq + 2 * POOL_HALO, WIDTH_C), F32)],
        compiler_params=_params(1),
        name="ctx_branches",
    )(q, k, v, za, ucs, zb, uc, zc, cmat, smat, wf, wp, ps)


def _merge_kernel(x_ref, mod_ref, ng_ref, wg_ref, ya_ref, yb_ref, yc_ref,
                  pa_ref, pb_ref, pc_ref, wo_ref, o_ref):
    x = x_ref[...]
    hb = _modulated_norm(x, mod_ref, ng_ref)
    gate = mod_ref[0, 2:3, :]

    def dots(y_ref, p_ref, col):
        g = jnp.dot(hb, wg_ref[0, :, col * D_MODEL:(col + 1) * D_MODEL], preferred_element_type=F32)
        return g, jnp.dot(y_ref[...], p_ref[0], preferred_element_type=F32)

    def gated(gy):
        return _sigmoid(gy[0]) * gy[1]

    da = dots(ya_ref, pa_ref, 0)
    db = dots(yb_ref, pb_ref, 1)
    m = gated(da)
    dc = dots(yc_ref, pc_ref, 2)
    m = m + gated(db)
    m = m + gated(dc)
    o_ref[...] = x + gate * jnp.dot(m.astype(BF16), wo_ref[0], preferred_element_type=F32)


def _merge(x2, mod, mod_seq, layer, ng, w_in, ya, yb, yc, pa, pb, pc, wo, tm=ROW_TILE):
    t = x2.shape[0]
    tiles_per_seq = mod_seq // tm
    row = lambda w: pl.BlockSpec((tm, w), lambda i: (i, 0))
    return pl.pallas_call(
        _merge_kernel,
        grid=(t // tm,),
        in_specs=[
            row(D_MODEL),
            pl.BlockSpec((1, 3, D_MODEL), lambda i: (i // tiles_per_seq, 0, 0)),
            _const_spec((1, D_MODEL)),
            pl.BlockSpec((1, D_MODEL, 3 * D_MODEL), lambda i: (layer, 0, 1)),
            row(WIDTH_A), row(WIDTH_B), row(WIDTH_C),
            _layer_spec((WIDTH_A, D_MODEL), layer), _layer_spec((WIDTH_B, D_MODEL), layer),
            _layer_spec((WIDTH_C, D_MODEL), layer), _layer_spec((D_MODEL, D_MODEL), layer),
        ],
        out_specs=row(D_MODEL),
        out_shape=jax.ShapeDtypeStruct((t, D_MODEL), F32),
        compiler_params=_params(1),
        name="merge",
    )(x2, mod, ng, w_in, ya, yb, yc, pa, pb, pc, wo)


def _block_diag(w):
    g, c, e = w.shape
    eye = jnp.eye(g, dtype=w.dtype)
    return (eye[:, None, :, None] * w[:, :, None, :]).reshape(g * c, g * e)


def kernel(x_prompt, x_sample, cache_k, cache_v, c, c_ctx, norm_g, w_ada, b_ada, w_in, q_norm_g, k_norm_g,
           rpb, w_fnet, w_pool, pool_scale, p_a, p_b, p_c, w_o):
    nb_p, seq_p, _ = x_prompt.shape
    nb_s, seq_s, _ = x_sample.shape

    cond = jnp.concatenate([c_ctx[None, :], c, jnp.zeros((8 - 1 - nb_s, D_MODEL), F32)], axis=0)
    ada = _ada(cond, w_ada, b_ada).reshape(DEPTH, 8, 3, D_MODEL)
    bias = _bias_table(rpb, seq_s // GRID_W)

    head_mean = jnp.asarray(np.kron(np.eye(N_HEADS), np.full((HEAD_DIM, HEAD_DIM), 1.0 / HEAD_DIM)), BF16)
    bd = jnp.asarray(_group_dft_const()).astype(BF16)
    dft_p = [jnp.asarray(m).astype(BF16) for m in _dft_consts(seq_p)]
    dft_s = [jnp.asarray(m).astype(BF16) for m in _dft_consts(seq_s)]

    w_in_b, wf_b = w_in.astype(BF16), w_fnet.astype(BF16)
    pa, pb, pc, wo = p_a.astype(BF16), p_b.astype(BF16), p_c.astype(BF16), w_o.astype(BF16)

    xp = x_prompt.reshape(nb_p * seq_p, D_MODEL)
    xs = x_sample.reshape(nb_s * seq_s, D_MODEL)
    new_kv = None
    for l in range(DEPTH):
        ng = norm_g[l][None, :]
        qg = jnp.tile(q_norm_g[l], N_HEADS)[None, :]
        kg = jnp.tile(k_norm_g[l], N_HEADS)[None, :]
        wp = _block_diag(w_pool[l]).astype(BF16)
        ps = pool_scale[l][None, :]
        mod_p = ada[l, 0:1]
        mod_s = ada[l, 1:1 + nb_s]

        (q, k, v, za, ucs, zb, uc, zc, kf, vf) = _inproj(
            xp, mod_p, seq_p, nb_p * seq_p, l, ng, w_in_b, qg, kg, head_mean, bd, prev_kv=new_kv, kv_cache=True)
        new_kv = (kf, vf)
        ya, yb, yc = _ctx_branches(q, k, v, za, ucs, zb, uc, zc, dft_p[0], dft_p[1], wf_b, l, wp, ps, seq_p)
        xp = _merge(xp, mod_p, nb_p * seq_p, l, ng, w_in_b, ya, yb, yc, pa, pb, pc, wo)

        (q, k, v, za, ucs, zb, uc, zc) = _inproj(
            xs, mod_s, seq_s, seq_s, l, ng, w_in_b, qg, kg, head_mean, bd)
        ya = _na_attn(q, k, v, cache_k, cache_v, l, bias, za, seq_s)
        yb = _fourier(ucs, zb, dft_s[0], dft_s[1], wf_b, l, seq_s, tk=512)
        yc = _pool(uc, zc, wp, ps, seq_s)
        xs = _merge(xs, mod_s, seq_s, l, ng, w_in_b, ya, yb, yc, pa, pb, pc, wo)

    return (xp.reshape(nb_p, seq_p, D_MODEL), xs.reshape(nb_s, seq_s, D_MODEL), new_kv[0], new_kv[1])
```

```python
import functools

import numpy as np
import jax
import jax.numpy as jnp
from jax.experimental import pallas as pl
from jax.experimental.pallas import tpu as pltpu

D_MODEL = 1024
DEPTH = 2
GRID_W = 64
HEAD_DIM = 64
WIDTH_A = 512
N_HEADS = 8
WIN_R = 8
WIN_C = 16
WIDTH_B = 256
N_GROUPS_B = 4
GROUP_B = 64
WIDTH_C = 256
POOL_WINDOWS = (2, 4, 8, 16)
GROUP_C = 64
EPS = 1e-6
NEG_INF = -1e30
LOG2E = 1.4426950408889634
Q_SCALE = HEAD_DIM ** -0.5 * LOG2E
BASE_W = 4 * WIDTH_A + 2 * WIDTH_B + 2 * WIDTH_C
IN_WIDTH = BASE_W + 3 * D_MODEL

OFF_Q, OFF_K, OFF_V, OFF_ZA = 0, 512, 1024, 1536
OFF_UB, OFF_ZB, OFF_UC, OFF_ZC = 2048, 2304, 2560, 2816
assert BASE_W == 3 * D_MODEL

POOL_HALO = 16
POOL_TILE = 128
ROW_TILE = 512
VMEM_LIMIT = 56 * 1024 * 1024

BF16 = jnp.bfloat16
F32 = jnp.float32


def _params(n_axes, vmem=VMEM_LIMIT):
    return pltpu.CompilerParams(dimension_semantics=("arbitrary",) * n_axes, vmem_limit_bytes=vmem)


def _const_spec(shape):
    zeros = (0,) * len(shape)
    return pl.BlockSpec(shape, lambda *_: zeros)


def _sigmoid(x):
    return 1.0 / (1.0 + jnp.exp(-x))


def _ada_kernel(cond_ref, w_ref, b_ref, o_ref):
    cnd = cond_ref[...]
    a = cnd * _sigmoid(cnd)
    o_ref[0] = jnp.dot(a, w_ref[0], preferred_element_type=F32) + b_ref[0]


def _ada(cond, w_ada, b_ada):
    tn = 1024
    return pl.pallas_call(
        _ada_kernel,
        grid=(DEPTH, 3 * D_MODEL // tn),
        in_specs=[
            pl.BlockSpec((8, D_MODEL), lambda l, j: (0, 0)),
            pl.BlockSpec((1, D_MODEL, tn), lambda l, j: (l, 0, j)),
            pl.BlockSpec((1, 1, tn), lambda l, j: (l, 0, j)),
        ],
        out_specs=pl.BlockSpec((1, 8, tn), lambda l, j: (l, 0, j)),
        out_shape=jax.ShapeDtypeStruct((DEPTH, 8, 3 * D_MODEL), F32),
        compiler_params=_params(2),
        name="ada",
    )(cond, w_ada, b_ada.reshape(DEPTH, 1, 3 * D_MODEL))


def _modulated_norm(x, mod_ref, ng_ref):
    shift = mod_ref[0, 0, 0:1, :]
    scale = mod_ref[0, 0, 1:2, :]
    y = x * jax.lax.rsqrt(jnp.mean(x * x, axis=-1, keepdims=True) + EPS)
    return ((y * ng_ref[0]) * (1.0 + scale) + shift).astype(BF16)


def _inproj_kernel(x_ref, mod_ref, ng_ref, w_ref, qg_ref, kg_ref, hm_ref, bd_ref, *rest, layer, kv_cache):
    n_prev = 2 if (kv_cache and layer > 0) else 0
    prev, outs = rest[:n_prev], rest[n_prev:]
    if kv_cache:
        (q_o, k_o, v_o, za_o, ucs_o, zb_o, uc_o, zc_o, kf_o, vf_o) = outs
    else:
        (q_o, k_o, v_o, za_o, ucs_o, zb_o, uc_o, zc_o) = outs
    hb = _modulated_norm(x_ref[...], mod_ref, ng_ref)

    def proj(off, width):
        return jnp.dot(hb, w_ref[0, :, off:off + width], preferred_element_type=F32)

    def head_norm(t, g_ref):
        ms = jnp.dot((t * t).astype(BF16), hm_ref[...], preferred_element_type=F32)
        return t * jax.lax.rsqrt(ms + EPS) * g_ref[0]

    pq = proj(OFF_Q, WIDTH_A)
    pk = proj(OFF_K, WIDTH_A)
    q = head_norm(pq, qg_ref)
    q_o[...] = (q * Q_SCALE).astype(BF16)
    v = proj(OFF_V, WIDTH_A)
    k = head_norm(pk, kg_ref)
    k_o[...] = k.astype(BF16)
    za = proj(OFF_ZA, WIDTH_A)
    v_o[...] = v.astype(BF16)
    if kv_cache:
        seq = kf_o.shape[3]
        if layer > 0:
            kf_o[:, 0:layer] = prev[0][...]
            vf_o[:, 0:layer] = prev[1][...]
        for si in range(kf_o.shape[0]):
            for hh in range(N_HEADS):
                rsl, csl = slice(si * seq, (si + 1) * seq), slice(hh * HEAD_DIM, (hh + 1) * HEAD_DIM)
                kf_o[si, layer, hh] = k[rsl, csl]
                vf_o[si, layer, hh] = v[rsl, csl]
    ub = proj(OFF_UB, WIDTH_B)
    za_o[...] = (za * _sigmoid(za)).astype(BF16)
    zb = proj(OFF_ZB, WIDTH_B)
    ucs_o[...] = jnp.dot(ub.astype(BF16), bd_ref[...], preferred_element_type=F32).astype(BF16)
    uc = proj(OFF_UC, WIDTH_C)
    zb_o[...] = (zb * _sigmoid(zb)).astype(BF16)
    zc = proj(OFF_ZC, WIDTH_C)
    uc_o[...] = uc
    zc_o[...] = (zc * _sigmoid(zc)).astype(BF16)


def _mod_spec(layer, row0, tiles_per_row):
    return pl.BlockSpec((1, 1, 3, D_MODEL), lambda i: (layer, row0 + i // tiles_per_row, 0, 0))


def _layer_spec(shape, layer):
    zeros = (0,) * len(shape)
    return pl.BlockSpec((1,) + tuple(shape), lambda *_: (layer,) + zeros)


def _inproj(x2, mod, mod_row, seq, mod_seq, layer, ng, w_in, qg, kg, hm, bd, prev_kv=None, kv_cache=False, tm=ROW_TILE):
    t = x2.shape[0]
    tiles_per_mod = mod_seq // tm
    row = lambda w: pl.BlockSpec((tm, w), lambda i: (i, 0))
    widths = [(WIDTH_A, BF16)] * 4 + [(2 * WIDTH_B, BF16), (WIDTH_B, BF16), (WIDTH_C, F32), (WIDTH_C, BF16)]
    in_specs = [
        row(D_MODEL),
        _mod_spec(layer, mod_row, tiles_per_mod),
        _layer_spec((1, D_MODEL), layer),
        pl.BlockSpec((1, D_MODEL, BASE_W), lambda i: (layer, 0, 0)),
        _layer_spec((1, WIDTH_A), layer),
        _layer_spec((1, WIDTH_A), layer),
        _const_spec((WIDTH_A, WIDTH_A)),
        _const_spec((WIDTH_B, 2 * WIDTH_B)),
    ]
    args = [x2, mod, ng, w_in, qg, kg, hm, bd]
    out_specs = [row(w) for w, _ in widths]
    out_shape = [jax.ShapeDtypeStruct((t, w), dt) for w, dt in widths]
    if kv_cache:
        assert tm % seq == 0
        cache = lambda n: pl.BlockSpec((tm // seq, n, N_HEADS, seq, HEAD_DIM), lambda i: (i, 0, 0, 0, 0))
        if layer > 0:
            in_specs += [cache(layer)] * 2
            args += list(prev_kv)
        out_specs += [cache(layer + 1)] * 2
        out_shape += [jax.ShapeDtypeStruct((t // seq, layer + 1, N_HEADS, seq, HEAD_DIM), F32)] * 2
    return pl.pallas_call(
        functools.partial(_inproj_kernel, layer=layer, kv_cache=kv_cache),
        grid=(t // tm,),
        in_specs=in_specs,
        out_specs=out_specs,
        out_shape=out_shape,
        compiler_params=_params(1),
        name="inproj",
    )(*args)


def _nt_dot(a, b):
    return jax.lax.dot_general(a, b, (((1,), (1,)), ((), ())), preferred_element_type=F32)


def _pair_normalise(res, low):
    num = jnp.where(low, res[0], res[1])
    den = pltpu.roll(jnp.where(low, res[1], res[0]), HEAD_DIM, 1)
    return num / den


def _ctx_attention(q_ref, k_ref, v_ref, za_ref, o_ref):
    low = jax.lax.broadcasted_iota(jnp.int32, (q_ref.shape[0], 128), 1) < HEAD_DIM

    def own(h):
        return low if h % 2 == 0 else jnp.logical_not(low)

    def scores(h):
        lsl = slice((h // 2) * 128, (h // 2 + 1) * 128)
        qs = q_ref[:, lsl]
        return _nt_dot(jnp.where(own(h), qs, jnp.zeros_like(qs)), k_ref[:, lsl])

    sc, res = {}, {}
    for t in range(N_HEADS + 1):
        if t < N_HEADS:
            sc[t] = scores(t)
        if t >= 1:
            h = t - 1
            lsl = slice((h // 2) * 128, (h // 2 + 1) * 128)
            s = sc.pop(h)
            p = jnp.exp2(s - jnp.max(s, axis=-1, keepdims=True)).astype(BF16)
            vs = v_ref[:, lsl]
            res[h] = jnp.dot(p, jnp.where(own(h), vs, jnp.ones_like(vs)), preferred_element_type=F32)
            if h % 2 == 1:
                pair = [res.pop(h - 1), res.pop(h)]
                o_ref[:, lsl] = (_pair_normalise(pair, low) * za_ref[:, lsl].astype(F32)).astype(BF16)


NA_ROWS = 4
NA_KROWS = WIN_R + NA_ROWS - 1
NA_Q = NA_ROWS * GRID_W
NA_K = NA_KROWS * GRID_W


def _na_union_start(i, rows):
    return jnp.clip(NA_ROWS * i - WIN_R // 2, 0, rows - NA_KROWS)


def _na_cases(rows):
    cases, step_case = [], []
    for i in range(rows // NA_ROWS):
        u0 = min(max(NA_ROWS * i - WIN_R // 2, 0), rows - NA_KROWS)
        los = tuple(min(max(NA_ROWS * i + j - WIN_R // 2, 0), rows - WIN_R) - u0 for j in range(NA_ROWS))
        case = (u0 - NA_ROWS * i + WIN_R - 1, los)
        if case not in cases:
            cases.append(case)
        step_case.append(cases.index(case))
    return cases, step_case


def _select_case(idx, values):
    out = values[-1]
    for t in range(len(values) - 2, -1, -1):
        out = jnp.where(idx == t, values[t], out)
    return out


def _fill_bias(src_ref, bias_ref, rows):
    cases, _ = _na_cases(rows)
    qc = jax.lax.broadcasted_iota(jnp.int32, (GRID_W, 128), 0)
    lane = jax.lax.broadcasted_iota(jnp.int32, (GRID_W, 128), 1)
    upper = lane >= GRID_W
    kc = jnp.where(upper, lane - GRID_W, lane)
    cs = jnp.clip(qc - WIN_C // 2, 0, GRID_W - WIN_C)
    col_ok = (kc >= cs) & (kc < cs + WIN_C)

    def per_head(hh, carry):
        for ci, (c, los) in enumerate(cases):
            for j in range(NA_ROWS):
                for p in range((NA_KROWS + 1) // 2):
                    sidx = min(max(2 * p - j + c + 1, 0), 15)
                    row = src_ref[0, hh, sidx:sidx + 1, :]
                    t = pltpu.roll(jnp.broadcast_to(row, (GRID_W, 128)), 0, 1, stride=1, stride_axis=0)
                    ok_lo = los[j] <= 2 * p < los[j] + WIN_R
                    ok_hi = los[j] <= 2 * p + 1 < los[j] + WIN_R
                    if ok_lo and ok_hi:
                        ok = col_ok
                    elif ok_lo:
                        ok = col_ok & jnp.logical_not(upper)
                    elif ok_hi:
                        ok = col_ok & upper
                    else:
                        ok = None
                    val = jnp.full((GRID_W, 128), NEG_INF, F32) if ok is None else jnp.where(ok, t * LOG2E, NEG_INF)
                    rsl = slice(j * GRID_W, (j + 1) * GRID_W)
                    if 2 * p + 1 < NA_KROWS:
                        bias_ref[ci, hh, rsl, p * 128:(p + 1) * 128] = val
                    else:
                        bias_ref[ci, hh, rsl, p * 128:p * 128 + GRID_W] = val[:, :GRID_W]
        return carry

    jax.lax.fori_loop(0, N_HEADS, per_head, 0)


def _bias_rows(rpb):
    nr = 2 * WIN_R - 1
    pad = jnp.pad(rpb, ((0, 0), (0, 0), (1, 1), (0, 0)))
    lo_half, hi_half = pad[:, :, 0:nr + 1], pad[:, :, 1:nr + 2]
    src = jnp.zeros((DEPTH, N_HEADS, nr + 1, 128), F32)
    src = src.at[..., 0:WIN_C].set(lo_half[..., WIN_C - 1:])
    src = src.at[..., 128 - (WIN_C - 1):].set(lo_half[..., :WIN_C - 1])
    return src.at[..., GRID_W - (WIN_C - 1):GRID_W + WIN_C].set(hi_half)


def _na_attn_kernel(q_ref, k_ref, v_ref, kc_ref, vc_ref, src_ref, za_ref, o_ref,
                    vaug_scr, kcp_scr, vcaug_scr, bias_ref, *, rows):
    i = pl.program_id(1)
    npair = N_HEADS // 2
    past = kc_ref.shape[3]
    case = _select_case(i, _na_cases(rows)[1])

    @pl.when((pl.program_id(0) == 0) & (i == 0))
    def _build_bias():
        _fill_bias(src_ref, bias_ref, rows)

    @pl.when(i == 0)
    def _prepare():
        low = jax.lax.broadcasted_iota(jnp.int32, (v_ref.shape[0], 128), 1) < HEAD_DIM
        ones_c = jnp.ones((past, HEAD_DIM), F32)
        for g in range(npair):
            vs = v_ref[:, g * 128:(g + 1) * 128]
            vaug_scr[0, g] = jnp.where(low, vs, jnp.ones_like(vs))
            vaug_scr[1, g] = jnp.where(low, jnp.ones_like(vs), vs)
            kcp_scr[g] = jnp.concatenate([kc_ref[0, 0, 2 * g], kc_ref[0, 0, 2 * g + 1]], axis=-1).astype(BF16)
            vcaug_scr[0, g] = jnp.concatenate([vc_ref[0, 0, 2 * g], ones_c], axis=-1).astype(BF16)
            vcaug_scr[1, g] = jnp.concatenate([ones_c, vc_ref[0, 0, 2 * g + 1]], axis=-1).astype(BF16)

    start = pl.multiple_of(_na_union_start(i, rows) * GRID_W, GRID_W)
    lowq = jax.lax.broadcasted_iota(jnp.int32, (NA_Q, 128), 1) < HEAD_DIM

    def scores(h):
        g, hh = divmod(h, 2)
        lsl = slice(g * 128, (g + 1) * 128)
        qs = q_ref[:, lsl]
        qz = jnp.where(lowq if hh == 0 else jnp.logical_not(lowq), qs, jnp.zeros_like(qs))
        s_w = _nt_dot(qz, k_ref[pl.ds(start, NA_K), lsl]) + bias_ref[case, h]
        s_c = _nt_dot(qz, kcp_scr[g])
        m = jnp.maximum(jnp.max(s_w, axis=-1, keepdims=True), jnp.max(s_c, axis=-1, keepdims=True))
        return s_w, s_c, m

    def probs(sc):
        s_w, s_c, m = sc
        return jnp.exp2(s_w - m).astype(BF16), jnp.exp2(s_c - m).astype(BF16)

    def weighted(h, pr):
        g, hh = divmod(h, 2)
        return (jnp.dot(pr[0], vaug_scr[hh, g, pl.ds(start, NA_K), :], preferred_element_type=F32)
                + jnp.dot(pr[1], vcaug_scr[hh, g], preferred_element_type=F32))

    sc, res = {}, {}
    for t in range(N_HEADS + 1):
        if t < N_HEADS:
            sc[t] = scores(t)
        if t >= 1:
            h = t - 1
            res[h] = weighted(h, probs(sc.pop(h)))
            if h % 2 == 1:
                lsl = slice((h // 2) * 128, (h // 2 + 1) * 128)
                pair = [res.pop(h - 1), res.pop(h)]
                o_ref[:, lsl] = (_pair_normalise(pair, lowq) * za_ref[:, lsl].astype(F32)).astype(BF16)


def _na_attn(q, k, v, cache_k, cache_v, layer, bias_rows, za, seq):
    t = q.shape[0]
    nb = t // seq
    rows = seq // GRID_W
    assert rows % NA_ROWS == 0 and rows >= NA_KROWS
    steps = rows // NA_ROWS
    ncase = len(_na_cases(rows)[0])
    past = cache_k.shape[3]
    qblk = pl.BlockSpec((NA_Q, WIDTH_A), lambda b, i: (b * steps + i, 0))
    kvblk = pl.BlockSpec((seq, WIDTH_A), lambda b, i: (b, 0))
    cblk = pl.BlockSpec((1, 1, N_HEADS, past, HEAD_DIM), lambda b, i: (b, layer, 0, 0, 0))
    return pl.pallas_call(
        functools.partial(_na_attn_kernel, rows=rows),
        grid=(nb, steps),
        in_specs=[qblk, kvblk, kvblk, cblk, cblk, _layer_spec(bias_rows.shape[1:], layer), qblk],
        out_specs=qblk,
        out_shape=jax.ShapeDtypeStruct((t, WIDTH_A), BF16),
        scratch_shapes=[
            pltpu.VMEM((2, N_HEADS // 2, seq, 128), BF16),
            pltpu.VMEM((N_HEADS // 2, past, 128), BF16),
            pltpu.VMEM((2, N_HEADS // 2, past, 128), BF16),
            pltpu.VMEM((ncase, N_HEADS, NA_Q, NA_K), F32),
        ],
        compiler_params=_params(2),
        name="na_attn",
    )(q, k, v, cache_k, cache_v, bias_rows, za)


def _fourier_kernel(c_ref, s_ref, ucs_ref, wf_ref, zb_ref, o_ref):
    uc = ucs_ref[:, 0:WIDTH_B]
    us = ucs_ref[:, WIDTH_B:2 * WIDTH_B]
    y = (jnp.dot(c_ref[...], uc, preferred_element_type=F32)
         - jnp.dot(s_ref[...], us, preferred_element_type=F32))
    y2 = jnp.dot(y.astype(BF16), wf_ref[0], preferred_element_type=F32)
    o_ref[...] = (y2 * zb_ref[...].astype(F32)).astype(BF16)


def _fourier(ucs, zb, cmat, smat, wf, layer, seq, tk):
    t = ucs.shape[0]
    nb = t // seq
    nk = seq // tk
    return pl.pallas_call(
        _fourier_kernel,
        grid=(nk, nb),
        in_specs=[
            pl.BlockSpec((tk, seq), lambda k, b: (k, 0)),
            pl.BlockSpec((tk, seq), lambda k, b: (k, 0)),
            pl.BlockSpec((seq, 2 * WIDTH_B), lambda k, b: (b, 0)),
            _layer_spec((WIDTH_B, WIDTH_B), layer),
            pl.BlockSpec((tk, WIDTH_B), lambda k, b: (b * nk + k, 0)),
        ],
        out_specs=pl.BlockSpec((tk, WIDTH_B), lambda k, b: (b * nk + k, 0)),
        out_shape=jax.ShapeDtypeStruct((t, WIDTH_B), BF16),
        compiler_params=_params(2),
        name="fourier",
    )(cmat, smat, ucs, wf, zb)


def _dft_consts(seq):
    n = np.arange(seq, dtype=np.int64)
    ang = 2.0 * np.pi * ((n[:, None] * n[None, :]) % seq).astype(np.float64) / seq
    nrm = 1.0 / np.sqrt(float(seq) * GROUP_B)
    cmat = (np.cos(ang) * nrm).astype(np.float32)
    smat = (np.sin(ang) * nrm).astype(np.float32)
    return cmat, smat


def _group_dft_const():
    m = np.arange(GROUP_B, dtype=np.int64)
    ang = 2.0 * np.pi * ((m[:, None] * m[None, :]) % GROUP_B).astype(np.float64) / GROUP_B
    eye = np.eye(N_GROUPS_B)
    return np.concatenate([np.kron(eye, np.cos(ang)), np.kron(eye, np.sin(ang))], axis=1).astype(np.float32)


def _pool_kernel(u_ref, zc_ref, w_ref, ps_ref, o_ref, pad_ref, *, seq):
    halo = POOL_HALO
    pad_ref[0:halo, :] = jnp.zeros((halo, WIDTH_C), F32)
    pad_ref[halo + seq:2 * halo + seq, :] = jnp.zeros((halo, WIDTH_C), F32)
    pad_ref[halo:halo + seq, :] = u_ref[...]
    ext = POOL_TILE + 2 * halo
    lane = jax.lax.broadcasted_iota(jnp.int32, (POOL_TILE, WIDTH_C), 1)
    g0, g1, g2 = lane < GROUP_C, lane < 2 * GROUP_C, lane < 3 * GROUP_C
    half = jnp.where(g0, POOL_WINDOWS[0] // 2,
                     jnp.where(g1, POOL_WINDOWS[1] // 2,
                               jnp.where(g2, POOL_WINDOWS[2] // 2, POOL_WINDOWS[3] // 2)))
    rowi = jax.lax.broadcasted_iota(jnp.int32, (POOL_TILE, WIDTH_C), 0)
    for ti in range(seq // POOL_TILE):
        t0 = ti * POOL_TILE
        x = pad_ref[t0:t0 + ext, :]
        a1 = x + pltpu.roll(x, 1, 0)
        w4 = pltpu.roll(a1, 1, 0) + pltpu.roll(a1, ext - 1, 0)
        w8 = pltpu.roll(w4, 2, 0) + pltpu.roll(w4, ext - 2, 0)
        w16 = pltpu.roll(w8, 4, 0) + pltpu.roll(w8, ext - 4, 0)
        mid = slice(halo, halo + POOL_TILE)
        wsum = jnp.where(g0, a1[mid], jnp.where(g1, w4[mid], jnp.where(g2, w8[mid], w16[mid])))
        tpos = rowi + t0
        cnt = (jnp.minimum(tpos + half, seq) - jnp.maximum(tpos - half, 0)).astype(F32)
        dlt = wsum / cnt - x[mid]
        y = jnp.dot(dlt.astype(BF16), w_ref[0], preferred_element_type=F32) * ps_ref[0]
        o_ref[t0:t0 + POOL_TILE, :] = (y * zc_ref[t0:t0 + POOL_TILE, :].astype(F32)).astype(BF16)


def _pool(uc, zc, wp, ps, layer, seq):
    t = uc.shape[0]
    blk = pl.BlockSpec((seq, WIDTH_C), lambda b: (b, 0))
    return pl.pallas_call(
        functools.partial(_pool_kernel, seq=seq),
        grid=(t // seq,),
        in_specs=[blk, blk, _layer_spec((WIDTH_C, WIDTH_C), layer), _layer_spec((1, WIDTH_C), layer)],
        out_specs=blk,
        out_shape=jax.ShapeDtypeStruct((t, WIDTH_C), BF16),
        scratch_shapes=[pltpu.VMEM((seq + 2 * POOL_HALO, WIDTH_C), F32)],
        compiler_params=_params(1),
        name="pool",
    )(uc, zc, wp, ps)


def _ctx_branches_kernel(q_ref, k_ref, v_ref, za_ref, ucs_ref, zb_ref, uc_ref, zc_ref,
                         c_ref, s_ref, wf_ref, wp_ref, ps_ref, ya_o, yb_o, yc_o, pad_ref, *, seq):
    _ctx_attention(q_ref, k_ref, v_ref, za_ref, ya_o)
    _fourier_kernel(c_ref, s_ref, ucs_ref, wf_ref, zb_ref, yb_o)
    _pool_kernel(uc_ref, zc_ref, wp_ref, ps_ref, yc_o, pad_ref, seq=seq)


def _ctx_branches(q, k, v, za, ucs, zb, uc, zc, cmat, smat, wf, layer, wp, ps, seq):
    t = q.shape[0]
    blk = lambda w: pl.BlockSpec((seq, w), lambda b: (b, 0))
    return pl.pallas_call(
        functools.partial(_ctx_branches_kernel, seq=seq),
        grid=(t // seq,),
        in_specs=[blk(WIDTH_A)] * 4 + [blk(2 * WIDTH_B), blk(WIDTH_B), blk(WIDTH_C), blk(WIDTH_C),
                                       _const_spec((seq, seq)), _const_spec((seq, seq)),
                                       _layer_spec((WIDTH_B, WIDTH_B), layer),
                                       _layer_spec((WIDTH_C, WIDTH_C), layer), _layer_spec((1, WIDTH_C), layer)],
        out_specs=[blk(WIDTH_A), blk(WIDTH_B), blk(WIDTH_C)],
        out_shape=[jax.ShapeDtypeStruct((t, w), BF16) for w in (WIDTH_A, WIDTH_B, WIDTH_C)],
        scratch_shapes=[pltpu.VMEM((seq + 2 * POOL_HALO, WIDTH_C), F32)],
        compiler_params=_params(1),
        name="ctx_branches",
    )(q, k, v, za, ucs, zb, uc, zc, cmat, smat, wf, wp, ps)


def _merge_kernel(x_ref, mod_ref, ng_ref, wg_ref, ya_ref, yb_ref, yc_ref,
                  pa_ref, pb_ref, pc_ref, wo_ref, o_ref):
    x = x_ref[...]
    hb = _modulated_norm(x, mod_ref, ng_ref)
    gate = mod_ref[0, 0, 2:3, :]

    def dots(y_ref, p_ref, col):
        g = jnp.dot(hb, wg_ref[0, :, col * D_MODEL:(col + 1) * D_MODEL], preferred_element_type=F32)
        return g, jnp.dot(y_ref[...], p_ref[0], preferred_element_type=F32)

    def gated(gy):
        return _sigmoid(gy[0]) * gy[1]

    da = dots(ya_ref, pa_ref, 0)
    db = dots(yb_ref, pb_ref, 1)
    m = gated(da)
    dc = dots(yc_ref, pc_ref, 2)
    m = m + gated(db)
    m = m + gated(dc)
    o_ref[...] = x + gate * jnp.dot(m.astype(BF16), wo_ref[0], preferred_element_type=F32)


def _merge(x2, mod, mod_row, mod_seq, layer, ng, w_in, ya, yb, yc, pa, pb, pc, wo, tm=ROW_TILE):
    t = x2.shape[0]
    tiles_per_seq = mod_seq // tm
    row = lambda w: pl.BlockSpec((tm, w), lambda i: (i, 0))
    return pl.pallas_call(
        _merge_kernel,
        grid=(t // tm,),
        in_specs=[
            row(D_MODEL),
            _mod_spec(layer, mod_row, tiles_per_seq),
            _layer_spec((1, D_MODEL), layer),
            pl.BlockSpec((1, D_MODEL, 3 * D_MODEL), lambda i: (layer, 0, 1)),
            row(WIDTH_A), row(WIDTH_B), row(WIDTH_C),
            _layer_spec((WIDTH_A, D_MODEL), layer), _layer_spec((WIDTH_B, D_MODEL), layer),
            _layer_spec((WIDTH_C, D_MODEL), layer), _layer_spec((D_MODEL, D_MODEL), layer),
        ],
        out_specs=row(D_MODEL),
        out_shape=jax.ShapeDtypeStruct((t, D_MODEL), F32),
        compiler_params=_params(1),
        name="merge",
    )(x2, mod, ng, w_in, ya, yb, yc, pa, pb, pc, wo)


def _block_diag(w):
    g, c, e = w.shape
    eye = jnp.eye(g, dtype=w.dtype)
    return (eye[:, None, :, None] * w[:, :, None, :]).reshape(g * c, g * e)


def kernel(x_prompt, x_sample, cache_k, cache_v, c, c_ctx, norm_g, w_ada, b_ada, w_in, q_norm_g, k_norm_g,
           rpb, w_fnet, w_pool, pool_scale, p_a, p_b, p_c, w_o):
    nb_p, seq_p, _ = x_prompt.shape
    nb_s, seq_s, _ = x_sample.shape

    cond = jnp.concatenate([c_ctx[None, :], c, jnp.zeros((8 - 1 - nb_s, D_MODEL), F32)], axis=0)
    ada = _ada(cond, w_ada, b_ada).reshape(DEPTH, 8, 3, D_MODEL)
    bias = _bias_rows(rpb)

    head_mean = jnp.asarray(np.kron(np.eye(N_HEADS), np.full((HEAD_DIM, HEAD_DIM), 1.0 / HEAD_DIM)), BF16)
    bd = jnp.asarray(_group_dft_const()).astype(BF16)
    dft_p = [jnp.asarray(m).astype(BF16) for m in _dft_consts(seq_p)]
    dft_s = [jnp.asarray(m).astype(BF16) for m in _dft_consts(seq_s)]

    w_in_b, wf_b = w_in.astype(BF16), w_fnet.astype(BF16)
    pa, pb, pc, wo = p_a.astype(BF16), p_b.astype(BF16), p_c.astype(BF16), w_o.astype(BF16)

    ng = norm_g[:, None, :]
    qg = jnp.tile(q_norm_g, (1, N_HEADS))[:, None, :]
    kg = jnp.tile(k_norm_g, (1, N_HEADS))[:, None, :]
    wp = jax.vmap(_block_diag)(w_pool).astype(BF16)
    ps = pool_scale[:, None, :]

    xp = x_prompt.reshape(nb_p * seq_p, D_MODEL)
    xs = x_sample.reshape(nb_s * seq_s, D_MODEL)
    new_kv = None
    for l in range(DEPTH):
        (q, k, v, za, ucs, zb, uc, zc, kf, vf) = _inproj(
            xp, ada, 0, seq_p, nb_p * seq_p, l, ng, w_in_b, qg, kg, head_mean, bd, prev_kv=new_kv, kv_cache=True)
        new_kv = (kf, vf)
        ya, yb, yc = _ctx_branches(q, k, v, za, ucs, zb, uc, zc, dft_p[0], dft_p[1], wf_b, l, wp, ps, seq_p)
        xp = _merge(xp, ada, 0, nb_p * seq_p, l, ng, w_in_b, ya, yb, yc, pa, pb, pc, wo)

        (q, k, v, za, ucs, zb, uc, zc) = _inproj(
            xs, ada, 1, seq_s, seq_s, l, ng, w_in_b, qg, kg, head_mean, bd)
        ya = _na_attn(q, k, v, cache_k, cache_v, l, bias, za, seq_s)
        yb = _fourier(ucs, zb, dft_s[0], dft_s[1], wf_b, l, seq_s, tk=512)
        yc = _pool(uc, zc, wp, ps, l, seq_s)
        xs = _merge(xs, ada, 1, seq_s, l, ng, w_in_b, ya, yb, yc, pa, pb, pc, wo)

    return (xp.reshape(nb_p, seq_p, D_MODEL), xs.reshape(nb_s, seq_s, D_MODEL), new_kv[0], new_kv[1])
```

```python
import functools

import numpy as np
import jax
import jax.numpy as jnp
from jax.experimental import pallas as pl
from jax.experimental.pallas import tpu as pltpu

D_MODEL = 1024
DEPTH = 2
GRID_W = 64
HEAD_DIM = 64
WIDTH_A = 512
N_HEADS = 8
WIN_R = 8
WIN_C = 16
WIDTH_B = 256
N_GROUPS_B = 4
GROUP_B = 64
WIDTH_C = 256
POOL_WINDOWS = (2, 4, 8, 16)
GROUP_C = 64
EPS = 1e-6
NEG_INF = -1e30
LOG2E = 1.4426950408889634
Q_SCALE = HEAD_DIM ** -0.5 * LOG2E
BASE_W = 4 * WIDTH_A + 2 * WIDTH_B + 2 * WIDTH_C
IN_WIDTH = BASE_W + 3 * D_MODEL

OFF_Q, OFF_K, OFF_V, OFF_ZA = 0, 512, 1024, 1536
OFF_UB, OFF_ZB, OFF_UC, OFF_ZC = 2048, 2304, 2560, 2816
assert BASE_W == 3 * D_MODEL

POOL_HALO = 16
POOL_TILE = 128
ROW_TILE = 512
VMEM_LIMIT = 56 * 1024 * 1024

BF16 = jnp.bfloat16
F32 = jnp.float32


def _params(n_axes, vmem=VMEM_LIMIT):
    return pltpu.CompilerParams(dimension_semantics=("arbitrary",) * n_axes, vmem_limit_bytes=vmem)


def _const_spec(shape):
    zeros = (0,) * len(shape)
    return pl.BlockSpec(shape, lambda *_: zeros)


def _sigmoid(x):
    return 1.0 / (1.0 + jnp.exp(-x))


def _ada_kernel(cond_ref, w_ref, b_ref, o_ref):
    cnd = cond_ref[...]
    a = cnd * _sigmoid(cnd)
    o_ref[0] = jnp.dot(a, w_ref[0], preferred_element_type=F32) + b_ref[0]


def _ada(cond, w_ada, b_ada):
    tn = 1024
    return pl.pallas_call(
        _ada_kernel,
        grid=(DEPTH, 3 * D_MODEL // tn),
        in_specs=[
            pl.BlockSpec((8, D_MODEL), lambda l, j: (0, 0)),
            pl.BlockSpec((1, D_MODEL, tn), lambda l, j: (l, 0, j)),
            pl.BlockSpec((1, 1, tn), lambda l, j: (l, 0, j)),
        ],
        out_specs=pl.BlockSpec((1, 8, tn), lambda l, j: (l, 0, j)),
        out_shape=jax.ShapeDtypeStruct((DEPTH, 8, 3 * D_MODEL), F32),
        compiler_params=_params(2),
        name="ada",
    )(cond, w_ada, b_ada.reshape(DEPTH, 1, 3 * D_MODEL))


def _modulated_norm(x, mod_ref, ng_ref):
    shift = mod_ref[0, 0, 0:1, :]
    scale = mod_ref[0, 0, 1:2, :]
    y = x * jax.lax.rsqrt(jnp.mean(x * x, axis=-1, keepdims=True) + EPS)
    return ((y * ng_ref[0]) * (1.0 + scale) + shift).astype(BF16)


def _inproj_kernel(x_ref, mod_ref, ng_ref, w_ref, qg_ref, kg_ref, hm_ref, bd_ref, *rest, layer, kv_cache):
    n_prev = 2 if (kv_cache and layer > 0) else 0
    prev, outs = rest[:n_prev], rest[n_prev:]
    if kv_cache:
        (q_o, k_o, v_o, za_o, ucs_o, zb_o, uc_o, zc_o, kf_o, vf_o) = outs
    else:
        (q_o, k_o, v_o, za_o, ucs_o, zb_o, uc_o, zc_o) = outs
    hb = _modulated_norm(x_ref[...], mod_ref, ng_ref)

    def proj(off, width):
        return jnp.dot(hb, w_ref[0, :, off:off + width], preferred_element_type=F32)

    def head_norm(t, g_ref):
        ms = jnp.dot((t * t).astype(BF16), hm_ref[...], preferred_element_type=F32)
        return t * jax.lax.rsqrt(ms + EPS) * g_ref[0]

    pq = proj(OFF_Q, WIDTH_A)
    pk = proj(OFF_K, WIDTH_A)
    q = head_norm(pq, qg_ref)
    q_o[...] = (q * Q_SCALE).astype(BF16)
    v = proj(OFF_V, WIDTH_A)
    k = head_norm(pk, kg_ref)
    k_o[...] = k.astype(BF16)
    za = proj(OFF_ZA, WIDTH_A)
    v_o[...] = v.astype(BF16)
    if kv_cache:
        seq = kf_o.shape[3]
        if layer > 0:
            kf_o[:, 0:layer] = prev[0][...]
            vf_o[:, 0:layer] = prev[1][...]
        for si in range(kf_o.shape[0]):
            for hh in range(N_HEADS):
                rsl, csl = slice(si * seq, (si + 1) * seq), slice(hh * HEAD_DIM, (hh + 1) * HEAD_DIM)
                kf_o[si, layer, hh] = k[rsl, csl]
                vf_o[si, layer, hh] = v[rsl, csl]
    ub = proj(OFF_UB, WIDTH_B)
    za_o[...] = (za * _sigmoid(za)).astype(BF16)
    zb = proj(OFF_ZB, WIDTH_B)
    ucs_o[...] = jnp.dot(ub.astype(BF16), bd_ref[...], preferred_element_type=F32).astype(BF16)
    uc = proj(OFF_UC, WIDTH_C)
    zb_o[...] = (zb * _sigmoid(zb)).astype(BF16)
    zc = proj(OFF_ZC, WIDTH_C)
    uc_o[...] = uc
    zc_o[...] = (zc * _sigmoid(zc)).astype(BF16)


def _mod_spec(layer, row0, tiles_per_row):
    return pl.BlockSpec((1, 1, 3, D_MODEL), lambda i: (layer, row0 + i // tiles_per_row, 0, 0))


def _layer_spec(shape, layer):
    zeros = (0,) * len(shape)
    return pl.BlockSpec((1,) + tuple(shape), lambda *_: (layer,) + zeros)


def _inproj(x2, mod, mod_row, seq, mod_seq, layer, ng, w_in, qg, kg, hm, bd, prev_kv=None, kv_cache=False, tm=ROW_TILE):
    t = x2.shape[0]
    tiles_per_mod = mod_seq // tm
    row = lambda w: pl.BlockSpec((tm, w), lambda i: (i, 0))
    widths = [(WIDTH_A, BF16)] * 4 + [(2 * WIDTH_B, BF16), (WIDTH_B, BF16), (WIDTH_C, F32), (WIDTH_C, BF16)]
    in_specs = [
        row(D_MODEL),
        _mod_spec(layer, mod_row, tiles_per_mod),
        _layer_spec((1, D_MODEL), layer),
        pl.BlockSpec((1, D_MODEL, BASE_W), lambda i: (layer, 0, 0)),
        _layer_spec((1, WIDTH_A), layer),
        _layer_spec((1, WIDTH_A), layer),
        _const_spec((WIDTH_A, WIDTH_A)),
        _const_spec((WIDTH_B, 2 * WIDTH_B)),
    ]
    args = [x2, mod, ng, w_in, qg, kg, hm, bd]
    out_specs = [row(w) for w, _ in widths]
    out_shape = [jax.ShapeDtypeStruct((t, w), dt) for w, dt in widths]
    if kv_cache:
        assert tm % seq == 0
        cache = lambda n: pl.BlockSpec((tm // seq, n, N_HEADS, seq, HEAD_DIM), lambda i: (i, 0, 0, 0, 0))
        if layer > 0:
            in_specs += [cache(layer)] * 2
            args += list(prev_kv)
        out_specs += [cache(layer + 1)] * 2
        out_shape += [jax.ShapeDtypeStruct((t // seq, layer + 1, N_HEADS, seq, HEAD_DIM), F32)] * 2
    return pl.pallas_call(
        functools.partial(_inproj_kernel, layer=layer, kv_cache=kv_cache),
        grid=(t // tm,),
        in_specs=in_specs,
        out_specs=out_specs,
        out_shape=out_shape,
        compiler_params=_params(1),
        name="inproj",
    )(*args)


def _nt_dot(a, b):
    return jax.lax.dot_general(a, b, (((1,), (1,)), ((), ())), preferred_element_type=F32)


def _pair_normalise(res, low):
    num = jnp.where(low, res[0], res[1])
    den = pltpu.roll(jnp.where(low, res[1], res[0]), HEAD_DIM, 1)
    return num / den


def _ctx_attention(q_ref, k_ref, v_ref, za_ref, o_ref):
    low = jax.lax.broadcasted_iota(jnp.int32, (q_ref.shape[0], 128), 1) < HEAD_DIM

    def own(h):
        return low if h % 2 == 0 else jnp.logical_not(low)

    def scores(h):
        lsl = slice((h // 2) * 128, (h // 2 + 1) * 128)
        qs = q_ref[:, lsl]
        return _nt_dot(jnp.where(own(h), qs, jnp.zeros_like(qs)), k_ref[:, lsl])

    sc, res = {}, {}
    for t in range(N_HEADS + 1):
        if t < N_HEADS:
            sc[t] = scores(t)
        if t >= 1:
            h = t - 1
            lsl = slice((h // 2) * 128, (h // 2 + 1) * 128)
            s = sc.pop(h)
            p = jnp.exp2(s - jnp.max(s, axis=-1, keepdims=True)).astype(BF16)
            vs = v_ref[:, lsl]
            res[h] = jnp.dot(p, jnp.where(own(h), vs, jnp.ones_like(vs)), preferred_element_type=F32)
            if h % 2 == 1:
                pair = [res.pop(h - 1), res.pop(h)]
                o_ref[:, lsl] = (_pair_normalise(pair, low) * za_ref[:, lsl].astype(F32)).astype(BF16)


NA_ROWS = 4
NA_KROWS = WIN_R + NA_ROWS - 1
NA_Q = NA_ROWS * GRID_W
NA_K = NA_KROWS * GRID_W


def _na_union_start(i, rows):
    return jnp.clip(NA_ROWS * i - WIN_R // 2, 0, rows - NA_KROWS)


def _na_cases(rows):
    cases, step_case = [], []
    for i in range(rows // NA_ROWS):
        u0 = min(max(NA_ROWS * i - WIN_R // 2, 0), rows - NA_KROWS)
        los = tuple(min(max(NA_ROWS * i + j - WIN_R // 2, 0), rows - WIN_R) - u0 for j in range(NA_ROWS))
        case = (u0 - NA_ROWS * i + WIN_R - 1, los)
        if case not in cases:
            cases.append(case)
        step_case.append(cases.index(case))
    return cases, step_case


def _select_case(idx, values):
    out = values[-1]
    for t in range(len(values) - 2, -1, -1):
        out = jnp.where(idx == t, values[t], out)
    return out


def _fill_bias(src_ref, bias_ref, rows):
    cases, _ = _na_cases(rows)
    qc = jax.lax.broadcasted_iota(jnp.int32, (GRID_W, 128), 0)
    lane = jax.lax.broadcasted_iota(jnp.int32, (GRID_W, 128), 1)
    upper = lane >= GRID_W
    kc = jnp.where(upper, lane - GRID_W, lane)
    cs = jnp.clip(qc - WIN_C // 2, 0, GRID_W - WIN_C)
    col_ok = (kc >= cs) & (kc < cs + WIN_C)

    def per_head(hh, carry):
        for ci, (c, los) in enumerate(cases):
            for j in range(NA_ROWS):
                for p in range((NA_KROWS + 1) // 2):
                    sidx = min(max(2 * p - j + c + 1, 0), 15)
                    row = src_ref[0, hh, sidx:sidx + 1, :]
                    t = pltpu.roll(jnp.broadcast_to(row, (GRID_W, 128)), 0, 1, stride=1, stride_axis=0)
                    ok_lo = los[j] <= 2 * p < los[j] + WIN_R
                    ok_hi = los[j] <= 2 * p + 1 < los[j] + WIN_R
                    if ok_lo and ok_hi:
                        ok = col_ok
                    elif ok_lo:
                        ok = col_ok & jnp.logical_not(upper)
                    elif ok_hi:
                        ok = col_ok & upper
                    else:
                        ok = None
                    val = jnp.full((GRID_W, 128), NEG_INF, F32) if ok is None else jnp.where(ok, t * LOG2E, NEG_INF)
                    rsl = slice(j * GRID_W, (j + 1) * GRID_W)
                    if 2 * p + 1 < NA_KROWS:
                        bias_ref[ci, hh, rsl, p * 128:(p + 1) * 128] = val
                    else:
                        bias_ref[ci, hh, rsl, p * 128:p * 128 + GRID_W] = val[:, :GRID_W]
        return carry

    jax.lax.fori_loop(0, N_HEADS, per_head, 0)


def _bias_rows(rpb):
    nr = 2 * WIN_R - 1
    pad = jnp.pad(rpb, ((0, 0), (0, 0), (1, 1), (0, 0)))
    lo_half, hi_half = pad[:, :, 0:nr + 1], pad[:, :, 1:nr + 2]
    src = jnp.zeros((DEPTH, N_HEADS, nr + 1, 128), F32)
    src = src.at[..., 0:WIN_C].set(lo_half[..., WIN_C - 1:])
    src = src.at[..., 128 - (WIN_C - 1):].set(lo_half[..., :WIN_C - 1])
    return src.at[..., GRID_W - (WIN_C - 1):GRID_W + WIN_C].set(hi_half)


def _na_attn_kernel(q_ref, k_ref, v_ref, kc_ref, vc_ref, src_ref, za_ref, o_ref,
                    vaug_scr, kcp_scr, vcaug_scr, bias_ref, *, rows):
    i = pl.program_id(1)
    npair = N_HEADS // 2
    past = kc_ref.shape[3]
    case = _select_case(i, _na_cases(rows)[1])

    @pl.when((pl.program_id(0) == 0) & (i == 0))
    def _build_bias():
        _fill_bias(src_ref, bias_ref, rows)

    @pl.when(i == 0)
    def _prepare():
        low = jax.lax.broadcasted_iota(jnp.int32, (v_ref.shape[0], 128), 1) < HEAD_DIM
        ones_c = jnp.ones((past, HEAD_DIM), F32)
        for g in range(npair):
            vs = v_ref[:, g * 128:(g + 1) * 128]
            vaug_scr[0, g] = jnp.where(low, vs, jnp.ones_like(vs))
            vaug_scr[1, g] = jnp.where(low, jnp.ones_like(vs), vs)
            kcp_scr[g] = jnp.concatenate([kc_ref[0, 0, 2 * g], kc_ref[0, 0, 2 * g + 1]], axis=-1).astype(BF16)
            vcaug_scr[0, g] = jnp.concatenate([vc_ref[0, 0, 2 * g], ones_c], axis=-1).astype(BF16)
            vcaug_scr[1, g] = jnp.concatenate([ones_c, vc_ref[0, 0, 2 * g + 1]], axis=-1).astype(BF16)

    start = pl.multiple_of(_na_union_start(i, rows) * GRID_W, GRID_W)
    lowq = jax.lax.broadcasted_iota(jnp.int32, (NA_Q, 128), 1) < HEAD_DIM

    def scores(h):
        g, hh = divmod(h, 2)
        lsl = slice(g * 128, (g + 1) * 128)
        qs = q_ref[:, lsl]
        qz = jnp.where(lowq if hh == 0 else jnp.logical_not(lowq), qs, jnp.zeros_like(qs))
        s_w = _nt_dot(qz, k_ref[pl.ds(start, NA_K), lsl]) + bias_ref[case, h]
        s_c = _nt_dot(qz, kcp_scr[g])
        m = jnp.maximum(jnp.max(s_w, axis=-1, keepdims=True), jnp.max(s_c, axis=-1, keepdims=True))
        return s_w, s_c, m

    def probs(sc):
        s_w, s_c, m = sc
        return jnp.exp2(s_w - m).astype(BF16), jnp.exp2(s_c - m).astype(BF16)

    def weighted(h, pr):
        g, hh = divmod(h, 2)
        return (jnp.dot(pr[0], vaug_scr[hh, g, pl.ds(start, NA_K), :], preferred_element_type=F32)
                + jnp.dot(pr[1], vcaug_scr[hh, g], preferred_element_type=F32))

    sc, res = {}, {}
    for t in range(N_HEADS + 1):
        if t < N_HEADS:
            sc[t] = scores(t)
        if t >= 1:
            h = t - 1
            res[h] = weighted(h, probs(sc.pop(h)))
            if h % 2 == 1:
                lsl = slice((h // 2) * 128, (h // 2 + 1) * 128)
                pair = [res.pop(h - 1), res.pop(h)]
                o_ref[:, lsl] = (_pair_normalise(pair, lowq) * za_ref[:, lsl].astype(F32)).astype(BF16)


def _na_attn(q, k, v, cache_k, cache_v, layer, bias_rows, za, seq):
    t = q.shape[0]
    nb = t // seq
    rows = seq // GRID_W
    assert rows % NA_ROWS == 0 and rows >= NA_KROWS
    steps = rows // NA_ROWS
    ncase = len(_na_cases(rows)[0])
    past = cache_k.shape[3]
    qblk = pl.BlockSpec((NA_Q, WIDTH_A), lambda b, i: (b * steps + i, 0))
    kvblk = pl.BlockSpec((seq, WIDTH_A), lambda b, i: (b, 0))
    cblk = pl.BlockSpec((1, 1, N_HEADS, past, HEAD_DIM), lambda b, i: (b, layer, 0, 0, 0))
    return pl.pallas_call(
        functools.partial(_na_attn_kernel, rows=rows),
        grid=(nb, steps),
        in_specs=[qblk, kvblk, kvblk, cblk, cblk, _layer_spec(bias_rows.shape[1:], layer), qblk],
        out_specs=qblk,
        out_shape=jax.ShapeDtypeStruct((t, WIDTH_A), BF16),
        scratch_shapes=[
            pltpu.VMEM((2, N_HEADS // 2, seq, 128), BF16),
            pltpu.VMEM((N_HEADS // 2, past, 128), BF16),
            pltpu.VMEM((2, N_HEADS // 2, past, 128), BF16),
            pltpu.VMEM((ncase, N_HEADS, NA_Q, NA_K), F32),
        ],
        compiler_params=_params(2),
        name="na_attn",
    )(q, k, v, cache_k, cache_v, bias_rows, za)


def _fourier_kernel(c_ref, s_ref, ucs_ref, wf_ref, zb_ref, o_ref):
    uc = ucs_ref[:, 0:WIDTH_B]
    us = ucs_ref[:, WIDTH_B:2 * WIDTH_B]
    y = (jnp.dot(c_ref[...], uc, preferred_element_type=F32)
         - jnp.dot(s_ref[...], us, preferred_element_type=F32))
    y2 = jnp.dot(y.astype(BF16), wf_ref[0], preferred_element_type=F32)
    o_ref[...] = (y2 * zb_ref[...].astype(F32)).astype(BF16)


MIRROR_BLOCK = 256


def _fourier_sym_kernel(ch_ref, sh_ref, cmid_ref, rev_ref, ucs_ref, wf_ref, zb_ref, o_ref, ext_scr, y_scr):
    half = ch_ref.shape[0]
    mb = MIRROR_BLOCK
    uc = ucs_ref[:, 0:WIDTH_B]
    us = ucs_ref[:, WIDTH_B:2 * WIDTH_B]
    a = jnp.dot(ch_ref[...], uc, preferred_element_type=F32)
    b = jnp.dot(sh_ref[...], us, preferred_element_type=F32)
    y_scr[0:half] = (a - b).astype(BF16)
    ext_scr[0:half] = (a + b).astype(BF16)
    nmid = cmid_ref.shape[0]
    ext_scr[half:half + nmid] = jnp.dot(cmid_ref[...], uc, preferred_element_type=F32).astype(BF16)
    ext_scr[half + nmid:half + mb] = jnp.zeros((mb - nmid, WIDTH_B), BF16)
    for blk in range(half // mb):
        lo = half - mb * (blk + 1)
        y_scr[half + mb * blk:half + mb * (blk + 1)] = jnp.dot(
            rev_ref[...], ext_scr[lo:lo + 2 * mb], preferred_element_type=F32).astype(BF16)
    y2 = jnp.dot(y_scr[...], wf_ref[0], preferred_element_type=F32)
    o_ref[...] = (y2 * zb_ref[...].astype(F32)).astype(BF16)


def _lat_branches_kernel(ch_ref, sh_ref, cmid_ref, rev_ref, ucs_ref, wf_ref, zb_ref, uc_ref, zc_ref, wp_ref, ps_ref,
                         yb_o, yc_o, ext_scr, y_scr, pad_ref, *, seq):
    _fourier_sym_kernel(ch_ref, sh_ref, cmid_ref, rev_ref, ucs_ref, wf_ref, zb_ref, yb_o, ext_scr, y_scr)
    _pool_kernel(uc_ref, zc_ref, wp_ref, ps_ref, yc_o, pad_ref, seq=seq)


def _lat_branches(ucs, zb, uc, zc, sym, wf, wp, ps, layer, seq):
    t = ucs.shape[0]
    half = seq // 2
    assert half % MIRROR_BLOCK == 0
    blk = lambda w: pl.BlockSpec((seq, w), lambda b: (b, 0))
    ch, sh, cmid, rev = sym
    return pl.pallas_call(
        functools.partial(_lat_branches_kernel, seq=seq),
        grid=(t // seq,),
        in_specs=[_const_spec((half, seq)), _const_spec((half, seq)), _const_spec((16, seq)),
                  _const_spec((MIRROR_BLOCK, 2 * MIRROR_BLOCK)),
                  blk(2 * WIDTH_B), _layer_spec((WIDTH_B, WIDTH_B), layer), blk(WIDTH_B),
                  blk(WIDTH_C), blk(WIDTH_C), _layer_spec((WIDTH_C, WIDTH_C), layer), _layer_spec((1, WIDTH_C), layer)],
        out_specs=[blk(WIDTH_B), blk(WIDTH_C)],
        out_shape=[jax.ShapeDtypeStruct((t, WIDTH_B), BF16), jax.ShapeDtypeStruct((t, WIDTH_C), BF16)],
        scratch_shapes=[pltpu.VMEM((half + MIRROR_BLOCK, WIDTH_B), BF16), pltpu.VMEM((seq, WIDTH_B), BF16),
                        pltpu.VMEM((seq + 2 * POOL_HALO, WIDTH_C), F32)],
        compiler_params=_params(1),
        name="lat_branches",
    )(ch, sh, cmid, rev, ucs, wf, zb, uc, zc, wp, ps)


def _dft_sym_consts(seq):
    cmat, smat = _dft_consts(seq)
    half = seq // 2
    rev = np.zeros((MIRROR_BLOCK, 2 * MIRROR_BLOCK), np.float32)
    rev[np.arange(MIRROR_BLOCK), MIRROR_BLOCK - np.arange(MIRROR_BLOCK)] = 1.0
    return cmat[:half], smat[:half], np.tile(cmat[half:half + 1], (16, 1)), rev


def _dft_consts(seq):
    n = np.arange(seq, dtype=np.int64)
    ang = 2.0 * np.pi * ((n[:, None] * n[None, :]) % seq).astype(np.float64) / seq
    nrm = 1.0 / np.sqrt(float(seq) * GROUP_B)
    cmat = (np.cos(ang) * nrm).astype(np.float32)
    smat = (np.sin(ang) * nrm).astype(np.float32)
    return cmat, smat


def _group_dft_const():
    m = np.arange(GROUP_B, dtype=np.int64)
    ang = 2.0 * np.pi * ((m[:, None] * m[None, :]) % GROUP_B).astype(np.float64) / GROUP_B
    eye = np.eye(N_GROUPS_B)
    return np.concatenate([np.kron(eye, np.cos(ang)), np.kron(eye, np.sin(ang))], axis=1).astype(np.float32)


def _pool_kernel(u_ref, zc_ref, w_ref, ps_ref, o_ref, pad_ref, *, seq):
    halo = POOL_HALO
    pad_ref[0:halo, :] = jnp.zeros((halo, WIDTH_C), F32)
    pad_ref[halo + seq:2 * halo + seq, :] = jnp.zeros((halo, WIDTH_C), F32)
    pad_ref[halo:halo + seq, :] = u_ref[...]
    ext = POOL_TILE + 2 * halo
    lane = jax.lax.broadcasted_iota(jnp.int32, (POOL_TILE, WIDTH_C), 1)
    g0, g1, g2 = lane < GROUP_C, lane < 2 * GROUP_C, lane < 3 * GROUP_C
    half = jnp.where(g0, POOL_WINDOWS[0] // 2,
                     jnp.where(g1, POOL_WINDOWS[1] // 2,
                               jnp.where(g2, POOL_WINDOWS[2] // 2, POOL_WINDOWS[3] // 2)))
    rowi = jax.lax.broadcasted_iota(jnp.int32, (POOL_TILE, WIDTH_C), 0)
    for ti in range(seq // POOL_TILE):
        t0 = ti * POOL_TILE
        x = pad_ref[t0:t0 + ext, :]
        a1 = x + pltpu.roll(x, 1, 0)
        w4 = pltpu.roll(a1, 1, 0) + pltpu.roll(a1, ext - 1, 0)
        w8 = pltpu.roll(w4, 2, 0) + pltpu.roll(w4, ext - 2, 0)
        w16 = pltpu.roll(w8, 4, 0) + pltpu.roll(w8, ext - 4, 0)
        mid = slice(halo, halo + POOL_TILE)
        wsum = jnp.where(g0, a1[mid], jnp.where(g1, w4[mid], jnp.where(g2, w8[mid], w16[mid])))
        tpos = rowi + t0
        cnt = (jnp.minimum(tpos + half, seq) - jnp.maximum(tpos - half, 0)).astype(F32)
        dlt = wsum / cnt - x[mid]
        y = jnp.dot(dlt.astype(BF16), w_ref[0], preferred_element_type=F32) * ps_ref[0]
        o_ref[t0:t0 + POOL_TILE, :] = (y * zc_ref[t0:t0 + POOL_TILE, :].astype(F32)).astype(BF16)


def _ctx_branches_kernel(q_ref, k_ref, v_ref, za_ref, ucs_ref, zb_ref, uc_ref, zc_ref,
                         c_ref, s_ref, wf_ref, wp_ref, ps_ref, ya_o, yb_o, yc_o, pad_ref, *, seq):
    _ctx_attention(q_ref, k_ref, v_ref, za_ref, ya_o)
    _fourier_kernel(c_ref, s_ref, ucs_ref, wf_ref, zb_ref, yb_o)
    _pool_kernel(uc_ref, zc_ref, wp_ref, ps_ref, yc_o, pad_ref, seq=seq)


def _ctx_branches(q, k, v, za, ucs, zb, uc, zc, cmat, smat, wf, layer, wp, ps, seq):
    t = q.shape[0]
    blk = lambda w: pl.BlockSpec((seq, w), lambda b: (b, 0))
    return pl.pallas_call(
        functools.partial(_ctx_branches_kernel, seq=seq),
        grid=(t // seq,),
        in_specs=[blk(WIDTH_A)] * 4 + [blk(2 * WIDTH_B), blk(WIDTH_B), blk(WIDTH_C), blk(WIDTH_C),
                                       _const_spec((seq, seq)), _const_spec((seq, seq)),
                                       _layer_spec((WIDTH_B, WIDTH_B), layer),
                                       _layer_spec((WIDTH_C, WIDTH_C), layer), _layer_spec((1, WIDTH_C), layer)],
        out_specs=[blk(WIDTH_A), blk(WIDTH_B), blk(WIDTH_C)],
        out_shape=[jax.ShapeDtypeStruct((t, w), BF16) for w in (WIDTH_A, WIDTH_B, WIDTH_C)],
        scratch_shapes=[pltpu.VMEM((seq + 2 * POOL_HALO, WIDTH_C), F32)],
        compiler_params=_params(1),
        name="ctx_branches",
    )(q, k, v, za, ucs, zb, uc, zc, cmat, smat, wf, wp, ps)


def _merge_kernel(x_ref, mod_ref, ng_ref, wg_ref, ya_ref, yb_ref, yc_ref,
                  pa_ref, pb_ref, pc_ref, wo_ref, o_ref):
    ba = jnp.dot(ya_ref[...], pa_ref[0], preferred_element_type=F32)
    bb = jnp.dot(yb_ref[...], pb_ref[0], preferred_element_type=F32)
    bc = jnp.dot(yc_ref[...], pc_ref[0], preferred_element_type=F32)
    x = x_ref[...]
    hb = _modulated_norm(x, mod_ref, ng_ref)
    gate = mod_ref[0, 0, 2:3, :]

    def gate_logits(col):
        return jnp.dot(hb, wg_ref[0, :, col * D_MODEL:(col + 1) * D_MODEL], preferred_element_type=F32)

    ga = gate_logits(0)
    gb = gate_logits(1)
    m = _sigmoid(ga) * ba
    gc = gate_logits(2)
    m = m + _sigmoid(gb) * bb
    m = m + _sigmoid(gc) * bc
    o_ref[...] = x + gate * jnp.dot(m.astype(BF16), wo_ref[0], preferred_element_type=F32)


def _merge(x2, mod, mod_row, mod_seq, layer, ng, w_in, ya, yb, yc, pa, pb, pc, wo, tm=ROW_TILE):
    t = x2.shape[0]
    tiles_per_seq = mod_seq // tm
    row = lambda w: pl.BlockSpec((tm, w), lambda i: (i, 0))
    return pl.pallas_call(
        _merge_kernel,
        grid=(t // tm,),
        in_specs=[
            row(D_MODEL),
            _mod_spec(layer, mod_row, tiles_per_seq),
            _layer_spec((1, D_MODEL), layer),
            pl.BlockSpec((1, D_MODEL, 3 * D_MODEL), lambda i: (layer, 0, 1)),
            row(WIDTH_A), row(WIDTH_B), row(WIDTH_C),
            _layer_spec((WIDTH_A, D_MODEL), layer), _layer_spec((WIDTH_B, D_MODEL), layer),
            _layer_spec((WIDTH_C, D_MODEL), layer), _layer_spec((D_MODEL, D_MODEL), layer),
        ],
        out_specs=row(D_MODEL),
        out_shape=jax.ShapeDtypeStruct((t, D_MODEL), F32),
        compiler_params=_params(1),
        name="merge",
    )(x2, mod, ng, w_in, ya, yb, yc, pa, pb, pc, wo)


def _block_diag(w):
    g, c, e = w.shape
    eye = jnp.eye(g, dtype=w.dtype)
    return (eye[:, None, :, None] * w[:, :, None, :]).reshape(g * c, g * e)


def kernel(x_prompt, x_sample, cache_k, cache_v, c, c_ctx, norm_g, w_ada, b_ada, w_in, q_norm_g, k_norm_g,
           rpb, w_fnet, w_pool, pool_scale, p_a, p_b, p_c, w_o):
    nb_p, seq_p, _ = x_prompt.shape
    nb_s, seq_s, _ = x_sample.shape

    cond = jnp.concatenate([c_ctx[None, :], c, jnp.zeros((8 - 1 - nb_s, D_MODEL), F32)], axis=0)
    ada = _ada(cond, w_ada, b_ada).reshape(DEPTH, 8, 3, D_MODEL)
    bias = _bias_rows(rpb)

    head_mean = jnp.asarray(np.kron(np.eye(N_HEADS), np.full((HEAD_DIM, HEAD_DIM), 1.0 / HEAD_DIM)), BF16)
    bd = jnp.asarray(_group_dft_const()).astype(BF16)
    dft_p = [jnp.asarray(m).astype(BF16) for m in _dft_consts(seq_p)]
    dft_s = [jnp.asarray(m).astype(BF16) for m in _dft_sym_consts(seq_s)]

    w_in_b, wf_b = w_in.astype(BF16), w_fnet.astype(BF16)
    pa, pb, pc, wo = p_a.astype(BF16), p_b.astype(BF16), p_c.astype(BF16), w_o.astype(BF16)

    ng = norm_g[:, None, :]
    qg = jnp.tile(q_norm_g, (1, N_HEADS))[:, None, :]
    kg = jnp.tile(k_norm_g, (1, N_HEADS))[:, None, :]
    wp = jax.vmap(_block_diag)(w_pool).astype(BF16)
    ps = pool_scale[:, None, :]

    xp = x_prompt.reshape(nb_p * seq_p, D_MODEL)
    xs = x_sample.reshape(nb_s * seq_s, D_MODEL)
    new_kv = None
    for l in range(DEPTH):
        (q, k, v, za, ucs, zb, uc, zc, kf, vf) = _inproj(
            xp, ada, 0, seq_p, nb_p * seq_p, l, ng, w_in_b, qg, kg, head_mean, bd, prev_kv=new_kv, kv_cache=True)
        new_kv = (kf, vf)
        ya, yb, yc = _ctx_branches(q, k, v, za, ucs, zb, uc, zc, dft_p[0], dft_p[1], wf_b, l, wp, ps, seq_p)
        xp = _merge(xp, ada, 0, nb_p * seq_p, l, ng, w_in_b, ya, yb, yc, pa, pb, pc, wo)

        (q, k, v, za, ucs, zb, uc, zc) = _inproj(
            xs, ada, 1, seq_s, seq_s, l, ng, w_in_b, qg, kg, head_mean, bd)
        ya = _na_attn(q, k, v, cache_k, cache_v, l, bias, za, seq_s)
        yb, yc = _lat_branches(ucs, zb, uc, zc, dft_s, wf_b, wp, ps, l, seq_s)
        xs = _merge(xs, ada, 1, seq_s, l, ng, w_in_b, ya, yb, yc, pa, pb, pc, wo)

    return (xp.reshape(nb_p, seq_p, D_MODEL), xs.reshape(nb_s, seq_s, D_MODEL), new_kv[0], new_kv[1])
```

```python
import functools

import numpy as np
import jax
import jax.numpy as jnp
from jax.experimental import pallas as pl
from jax.experimental.pallas import tpu as pltpu

D_MODEL = 1024
DEPTH = 2
GRID_W = 64
HEAD_DIM = 64
WIDTH_A = 512
N_HEADS = 8
WIN_R = 8
WIN_C = 16
WIDTH_B = 256
N_GROUPS_B = 4
GROUP_B = 64
WIDTH_C = 256
POOL_WINDOWS = (2, 4, 8, 16)
GROUP_C = 64
EPS = 1e-6
NEG_INF = -1e30
LOG2E = 1.4426950408889634
Q_SCALE = HEAD_DIM ** -0.5 * LOG2E
BASE_W = 4 * WIDTH_A + 2 * WIDTH_B + 2 * WIDTH_C
IN_WIDTH = BASE_W + 3 * D_MODEL

OFF_Q, OFF_K, OFF_V, OFF_ZA = 0, 512, 1024, 1536
OFF_UB, OFF_ZB, OFF_UC, OFF_ZC = 2048, 2304, 2560, 2816
assert BASE_W == 3 * D_MODEL

POOL_HALO = 16
POOL_TILE = 128
ROW_TILE = 512
VMEM_LIMIT = 56 * 1024 * 1024

BF16 = jnp.bfloat16
F32 = jnp.float32


def _params(n_axes, vmem=VMEM_LIMIT):
    return pltpu.CompilerParams(dimension_semantics=("arbitrary",) * n_axes, vmem_limit_bytes=vmem)


def _const_spec(shape):
    zeros = (0,) * len(shape)
    return pl.BlockSpec(shape, lambda *_: zeros)


def _sigmoid(x):
    return 1.0 / (1.0 + jnp.exp(-x))


def _ada_kernel(cond_ref, w_ref, b_ref, o_ref):
    cnd = cond_ref[...]
    a = cnd * _sigmoid(cnd)
    o_ref[0] = jnp.dot(a, w_ref[0], preferred_element_type=F32) + b_ref[0]


def _ada(cond, w_ada, b_ada):
    tn = 1024
    return pl.pallas_call(
        _ada_kernel,
        grid=(DEPTH, 3 * D_MODEL // tn),
        in_specs=[
            pl.BlockSpec((8, D_MODEL), lambda l, j: (0, 0)),
            pl.BlockSpec((1, D_MODEL, tn), lambda l, j: (l, 0, j)),
            pl.BlockSpec((1, 1, tn), lambda l, j: (l, 0, j)),
        ],
        out_specs=pl.BlockSpec((1, 8, tn), lambda l, j: (l, 0, j)),
        out_shape=jax.ShapeDtypeStruct((DEPTH, 8, 3 * D_MODEL), F32),
        compiler_params=_params(2),
        name="ada",
    )(cond, w_ada, b_ada.reshape(DEPTH, 1, 3 * D_MODEL))


def _modulated_norm(x, mod_ref, ng_ref):
    shift = mod_ref[0, 0, 0:1, :]
    scale = mod_ref[0, 0, 1:2, :]
    y = x * jax.lax.rsqrt(jnp.mean(x * x, axis=-1, keepdims=True) + EPS)
    return ((y * ng_ref[0]) * (1.0 + scale) + shift).astype(BF16)


def _inproj_kernel(x_ref, mod_ref, ng_ref, w_ref, qg_ref, kg_ref, hm_ref, bd_ref, *rest, layer, kv_cache):
    n_prev = 2 if (kv_cache and layer > 0) else 0
    prev, outs = rest[:n_prev], rest[n_prev:]
    if kv_cache:
        (q_o, k_o, v_o, za_o, ucs_o, zb_o, uc_o, zc_o, kf_o, vf_o) = outs
    else:
        (q_o, k_o, v_o, za_o, ucs_o, zb_o, uc_o, zc_o) = outs
    hb = _modulated_norm(x_ref[...], mod_ref, ng_ref)

    def proj(off, width):
        return jnp.dot(hb, w_ref[0, :, off:off + width], preferred_element_type=F32)

    def head_norm(t, g_ref):
        ms = jnp.dot((t * t).astype(BF16), hm_ref[...], preferred_element_type=F32)
        return t * jax.lax.rsqrt(ms + EPS) * g_ref[0]

    pq = proj(OFF_Q, WIDTH_A)
    pk = proj(OFF_K, WIDTH_A)
    q = head_norm(pq, qg_ref)
    q_o[...] = (q * Q_SCALE).astype(BF16)
    v = proj(OFF_V, WIDTH_A)
    k = head_norm(pk, kg_ref)
    k_o[...] = k.astype(BF16)
    za = proj(OFF_ZA, WIDTH_A)
    v_o[...] = v.astype(BF16)
    if kv_cache:
        seq = kf_o.shape[4]
        if layer > 0:
            kf_o[:, 0:layer] = prev[0][...]
            vf_o[:, 0:layer] = prev[1][...]
        for si in range(kf_o.shape[0]):
            for g in range(N_HEADS // 2):
                rsl, csl = slice(si * seq, (si + 1) * seq), slice(g * 128, (g + 1) * 128)
                kf_o[si, layer, 2 * g:2 * g + 2] = k[rsl, csl].T.reshape(2, HEAD_DIM, seq)
                vf_o[si, layer, 2 * g:2 * g + 2] = v[rsl, csl].T.reshape(2, HEAD_DIM, seq)
    ub = proj(OFF_UB, WIDTH_B)
    za_o[...] = (za * _sigmoid(za)).astype(BF16)
    zb = proj(OFF_ZB, WIDTH_B)
    ucs_o[...] = jnp.dot(ub.astype(BF16), bd_ref[...], preferred_element_type=F32).astype(BF16)
    uc = proj(OFF_UC, WIDTH_C)
    zb_o[...] = (zb * _sigmoid(zb)).astype(BF16)
    zc = proj(OFF_ZC, WIDTH_C)
    uc_o[...] = uc
    zc_o[...] = (zc * _sigmoid(zc)).astype(BF16)


def _mod_spec(layer, row0, tiles_per_row):
    return pl.BlockSpec((1, 1, 3, D_MODEL), lambda i: (layer, row0 + i // tiles_per_row, 0, 0))


def _layer_spec(shape, layer):
    zeros = (0,) * len(shape)
    return pl.BlockSpec((1,) + tuple(shape), lambda *_: (layer,) + zeros)


def _inproj(x2, mod, mod_row, seq, mod_seq, layer, ng, w_in, qg, kg, hm, bd, prev_kv=None, kv_cache=False, tm=ROW_TILE):
    t = x2.shape[0]
    tiles_per_mod = mod_seq // tm
    row = lambda w: pl.BlockSpec((tm, w), lambda i: (i, 0))
    widths = [(WIDTH_A, BF16)] * 4 + [(2 * WIDTH_B, BF16), (WIDTH_B, BF16), (WIDTH_C, F32), (WIDTH_C, BF16)]
    in_specs = [
        row(D_MODEL),
        _mod_spec(layer, mod_row, tiles_per_mod),
        _layer_spec((1, D_MODEL), layer),
        pl.BlockSpec((1, D_MODEL, BASE_W), lambda i: (layer, 0, 0)),
        _layer_spec((1, WIDTH_A), layer),
        _layer_spec((1, WIDTH_A), layer),
        _const_spec((WIDTH_A, WIDTH_A)),
        _const_spec((WIDTH_B, 2 * WIDTH_B)),
    ]
    args = [x2, mod, ng, w_in, qg, kg, hm, bd]
    out_specs = [row(w) for w, _ in widths]
    out_shape = [jax.ShapeDtypeStruct((t, w), dt) for w, dt in widths]
    if kv_cache:
        assert tm % seq == 0
        cache = lambda n: pl.BlockSpec((tm // seq, n, N_HEADS, HEAD_DIM, seq), lambda i: (i, 0, 0, 0, 0))
        if layer > 0:
            in_specs += [cache(layer)] * 2
            args += list(prev_kv)
        out_specs += [cache(layer + 1)] * 2
        out_shape += [jax.ShapeDtypeStruct((t // seq, layer + 1, N_HEADS, HEAD_DIM, seq), F32)] * 2
    return pl.pallas_call(
        functools.partial(_inproj_kernel, layer=layer, kv_cache=kv_cache),
        grid=(t // tm,),
        in_specs=in_specs,
        out_specs=out_specs,
        out_shape=out_shape,
        compiler_params=_params(1),
        name="inproj",
    )(*args)


def _nt_dot(a, b):
    return jax.lax.dot_general(a, b, (((1,), (1,)), ((), ())), preferred_element_type=F32)


def _pair_normalise(res, low):
    num = jnp.where(low, res[0], res[1])
    den = pltpu.roll(jnp.where(low, res[1], res[0]), HEAD_DIM, 1)
    return num / den


def _ctx_attention(q_ref, k_ref, v_ref, za_ref, o_ref):
    low = jax.lax.broadcasted_iota(jnp.int32, (q_ref.shape[0], 128), 1) < HEAD_DIM

    def own(h):
        return low if h % 2 == 0 else jnp.logical_not(low)

    def scores(h):
        lsl = slice((h // 2) * 128, (h // 2 + 1) * 128)
        qs = q_ref[:, lsl]
        return _nt_dot(jnp.where(own(h), qs, jnp.zeros_like(qs)), k_ref[:, lsl])

    sc, res = {}, {}
    for t in range(N_HEADS + 1):
        if t < N_HEADS:
            sc[t] = scores(t)
        if t >= 1:
            h = t - 1
            lsl = slice((h // 2) * 128, (h // 2 + 1) * 128)
            s = sc.pop(h)
            p = jnp.exp2(s - jnp.max(s, axis=-1, keepdims=True)).astype(BF16)
            vs = v_ref[:, lsl]
            res[h] = jnp.dot(p, jnp.where(own(h), vs, jnp.ones_like(vs)), preferred_element_type=F32)
            if h % 2 == 1:
                pair = [res.pop(h - 1), res.pop(h)]
                o_ref[:, lsl] = (_pair_normalise(pair, low) * za_ref[:, lsl].astype(F32)).astype(BF16)


NA_ROWS = 4
NA_KROWS = WIN_R + NA_ROWS - 1
NA_Q = NA_ROWS * GRID_W
NA_K = NA_KROWS * GRID_W


def _na_union_start(i, rows):
    return jnp.clip(NA_ROWS * i - WIN_R // 2, 0, rows - NA_KROWS)


def _na_cases(rows):
    cases, step_case = [], []
    for i in range(rows // NA_ROWS):
        u0 = min(max(NA_ROWS * i - WIN_R // 2, 0), rows - NA_KROWS)
        los = tuple(min(max(NA_ROWS * i + j - WIN_R // 2, 0), rows - WIN_R) - u0 for j in range(NA_ROWS))
        case = (u0 - NA_ROWS * i + WIN_R - 1, los)
        if case not in cases:
            cases.append(case)
        step_case.append(cases.index(case))
    return cases, step_case


def _select_case(idx, values):
    out = values[-1]
    for t in range(len(values) - 2, -1, -1):
        out = jnp.where(idx == t, values[t], out)
    return out


def _fill_bias(src_ref, bias_ref, rows):
    cases, _ = _na_cases(rows)
    qc = jax.lax.broadcasted_iota(jnp.int32, (GRID_W, 128), 0)
    lane = jax.lax.broadcasted_iota(jnp.int32, (GRID_W, 128), 1)
    upper = lane >= GRID_W
    kc = jnp.where(upper, lane - GRID_W, lane)
    cs = jnp.clip(qc - WIN_C // 2, 0, GRID_W - WIN_C)
    col_ok = (kc >= cs) & (kc < cs + WIN_C)

    def per_head(hh, carry):
        for ci, (c, los) in enumerate(cases):
            for j in range(NA_ROWS):
                for p in range((NA_KROWS + 1) // 2):
                    sidx = min(max(2 * p - j + c + 1, 0), 15)
                    row = src_ref[0, hh, sidx:sidx + 1, :]
                    t = pltpu.roll(jnp.broadcast_to(row, (GRID_W, 128)), 0, 1, stride=1, stride_axis=0)
                    ok_lo = los[j] <= 2 * p < los[j] + WIN_R
                    ok_hi = los[j] <= 2 * p + 1 < los[j] + WIN_R
                    if ok_lo and ok_hi:
                        ok = col_ok
                    elif ok_lo:
                        ok = col_ok & jnp.logical_not(upper)
                    elif ok_hi:
                        ok = col_ok & upper
                    else:
                        ok = None
                    val = jnp.full((GRID_W, 128), NEG_INF, F32) if ok is None else jnp.where(ok, t * LOG2E, NEG_INF)
                    rsl = slice(j * GRID_W, (j + 1) * GRID_W)
                    if 2 * p + 1 < NA_KROWS:
                        bias_ref[ci, hh, rsl, p * 128:(p + 1) * 128] = val
                    else:
                        bias_ref[ci, hh, rsl, p * 128:p * 128 + GRID_W] = val[:, :GRID_W]
        return carry

    jax.lax.fori_loop(0, N_HEADS, per_head, 0)


def _bias_rows(rpb):
    nr = 2 * WIN_R - 1
    pad = jnp.pad(rpb, ((0, 0), (0, 0), (1, 1), (0, 0)))
    lo_half, hi_half = pad[:, :, 0:nr + 1], pad[:, :, 1:nr + 2]
    src = jnp.zeros((DEPTH, N_HEADS, nr + 1, 128), F32)
    src = src.at[..., 0:WIN_C].set(lo_half[..., WIN_C - 1:])
    src = src.at[..., 128 - (WIN_C - 1):].set(lo_half[..., :WIN_C - 1])
    return src.at[..., GRID_W - (WIN_C - 1):GRID_W + WIN_C].set(hi_half)


def _na_attn_kernel(q_ref, k_ref, v_ref, kc_ref, vc_ref, src_ref, za_ref, o_ref,
                    vaug_scr, kcp_scr, vcaug_scr, bias_ref, *, rows):
    i = pl.program_id(1)
    npair = N_HEADS // 2
    past = kc_ref.shape[4]
    case = _select_case(i, _na_cases(rows)[1])

    @pl.when((pl.program_id(0) == 0) & (i == 0))
    def _build_bias():
        _fill_bias(src_ref, bias_ref, rows)

    @pl.when(i == 0)
    def _prepare():
        low = jax.lax.broadcasted_iota(jnp.int32, (v_ref.shape[0], 128), 1) < HEAD_DIM
        top = jax.lax.broadcasted_iota(jnp.int32, (128, past), 0) < HEAD_DIM
        for g in range(npair):
            vs = v_ref[:, g * 128:(g + 1) * 128]
            vaug_scr[0, g] = jnp.where(low, vs, jnp.ones_like(vs))
            vaug_scr[1, g] = jnp.where(low, jnp.ones_like(vs), vs)
            kcp_scr[g] = kc_ref[0, 0, 2 * g:2 * g + 2].reshape(128, past).astype(BF16)
            vct = vc_ref[0, 0, 2 * g:2 * g + 2].reshape(128, past)
            vcaug_scr[0, g] = jnp.where(top, vct, 1.0).astype(BF16)
            vcaug_scr[1, g] = jnp.where(top, 1.0, vct).astype(BF16)

    start = pl.multiple_of(_na_union_start(i, rows) * GRID_W, GRID_W)
    lowq = jax.lax.broadcasted_iota(jnp.int32, (NA_Q, 128), 1) < HEAD_DIM

    def scores(h):
        g, hh = divmod(h, 2)
        lsl = slice(g * 128, (g + 1) * 128)
        qs = q_ref[:, lsl]
        qz = jnp.where(lowq if hh == 0 else jnp.logical_not(lowq), qs, jnp.zeros_like(qs))
        s_w = _nt_dot(qz, k_ref[pl.ds(start, NA_K), lsl]) + bias_ref[case, h]
        s_c = jnp.dot(qz, kcp_scr[g], preferred_element_type=F32)
        m = jnp.maximum(jnp.max(s_w, axis=-1, keepdims=True), jnp.max(s_c, axis=-1, keepdims=True))
        return s_w, s_c, m

    def probs(sc):
        s_w, s_c, m = sc
        return jnp.exp2(s_w - m).astype(BF16), jnp.exp2(s_c - m).astype(BF16)

    def weighted(h, pr):
        g, hh = divmod(h, 2)
        return (jnp.dot(pr[0], vaug_scr[hh, g, pl.ds(start, NA_K), :], preferred_element_type=F32)
                + _nt_dot(pr[1], vcaug_scr[hh, g]))

    sc, res = {}, {}
    for t in range(N_HEADS + 1):
        if t < N_HEADS:
            sc[t] = scores(t)
        if t >= 1:
            h = t - 1
            res[h] = weighted(h, probs(sc.pop(h)))
            if h % 2 == 1:
                lsl = slice((h // 2) * 128, (h // 2 + 1) * 128)
                pair = [res.pop(h - 1), res.pop(h)]
                o_ref[:, lsl] = (_pair_normalise(pair, lowq) * za_ref[:, lsl].astype(F32)).astype(BF16)


def _na_attn(q, k, v, cache_k, cache_v, layer, bias_rows, za, seq):
    t = q.shape[0]
    nb = t // seq
    rows = seq // GRID_W
    assert rows % NA_ROWS == 0 and rows >= NA_KROWS
    steps = rows // NA_ROWS
    ncase = len(_na_cases(rows)[0])
    past = cache_k.shape[4]
    qblk = pl.BlockSpec((NA_Q, WIDTH_A), lambda b, i: (b * steps + i, 0))
    kvblk = pl.BlockSpec((seq, WIDTH_A), lambda b, i: (b, 0))
    cblk = pl.BlockSpec((1, 1, N_HEADS, HEAD_DIM, past), lambda b, i: (b, layer, 0, 0, 0))
    return pl.pallas_call(
        functools.partial(_na_attn_kernel, rows=rows),
        grid=(nb, steps),
        in_specs=[qblk, kvblk, kvblk, cblk, cblk, _layer_spec(bias_rows.shape[1:], layer), qblk],
        out_specs=qblk,
        out_shape=jax.ShapeDtypeStruct((t, WIDTH_A), BF16),
        scratch_shapes=[
            pltpu.VMEM((2, N_HEADS // 2, seq, 128), BF16),
            pltpu.VMEM((N_HEADS // 2, 128, past), BF16),
            pltpu.VMEM((2, N_HEADS // 2, 128, past), BF16),
            pltpu.VMEM((ncase, N_HEADS, NA_Q, NA_K), F32),
        ],
        compiler_params=_params(2),
        name="na_attn",
    )(q, k, v, cache_k, cache_v, bias_rows, za)


def _fourier_kernel(c_ref, s_ref, ucs_ref, wf_ref, zb_ref, o_ref):
    uc = ucs_ref[:, 0:WIDTH_B]
    us = ucs_ref[:, WIDTH_B:2 * WIDTH_B]
    y = (jnp.dot(c_ref[...], uc, preferred_element_type=F32)
         - jnp.dot(s_ref[...], us, preferred_element_type=F32))
    y2 = jnp.dot(y.astype(BF16), wf_ref[0], preferred_element_type=F32)
    o_ref[...] = (y2 * zb_ref[...].astype(F32)).astype(BF16)


MIRROR_BLOCK = 256


def _fourier_sym_kernel(ch_ref, sh_ref, cmid_ref, rev_ref, ucs_ref, wf_ref, zb_ref, o_ref, ext_scr, y_scr):
    half = ch_ref.shape[0]
    mb = MIRROR_BLOCK
    uc = ucs_ref[:, 0:WIDTH_B]
    us = ucs_ref[:, WIDTH_B:2 * WIDTH_B]
    a = jnp.dot(ch_ref[...], uc, preferred_element_type=F32)
    b = jnp.dot(sh_ref[...], us, preferred_element_type=F32)
    y_scr[0:half] = (a - b).astype(BF16)
    ext_scr[0:half] = (a + b).astype(BF16)
    nmid = cmid_ref.shape[0]
    ext_scr[half:half + nmid] = jnp.dot(cmid_ref[...], uc, preferred_element_type=F32).astype(BF16)
    ext_scr[half + nmid:half + mb] = jnp.zeros((mb - nmid, WIDTH_B), BF16)
    for blk in range(half // mb):
        lo = half - mb * (blk + 1)
        y_scr[half + mb * blk:half + mb * (blk + 1)] = jnp.dot(
            rev_ref[...], ext_scr[lo:lo + 2 * mb], preferred_element_type=F32).astype(BF16)
    y2 = jnp.dot(y_scr[...], wf_ref[0], preferred_element_type=F32)
    o_ref[...] = (y2 * zb_ref[...].astype(F32)).astype(BF16)


def _lat_branches_kernel(ch_ref, sh_ref, cmid_ref, rev_ref, ucs_ref, wf_ref, zb_ref, uc_ref, zc_ref, wp_ref, ps_ref,
                         yb_o, yc_o, ext_scr, y_scr, pad_ref, *, seq):
    _fourier_sym_kernel(ch_ref, sh_ref, cmid_ref, rev_ref, ucs_ref, wf_ref, zb_ref, yb_o, ext_scr, y_scr)
    _pool_kernel(uc_ref, zc_ref, wp_ref, ps_ref, yc_o, pad_ref, seq=seq)


def _lat_branches(ucs, zb, uc, zc, sym, wf, wp, ps, layer, seq):
    t = ucs.shape[0]
    half = seq // 2
    assert half % MIRROR_BLOCK == 0
    blk = lambda w: pl.BlockSpec((seq, w), lambda b: (b, 0))
    ch, sh, cmid, rev = sym
    return pl.pallas_call(
        functools.partial(_lat_branches_kernel, seq=seq),
        grid=(t // seq,),
        in_specs=[_const_spec((half, seq)), _const_spec((half, seq)), _const_spec((16, seq)),
                  _const_spec((MIRROR_BLOCK, 2 * MIRROR_BLOCK)),
                  blk(2 * WIDTH_B), _layer_spec((WIDTH_B, WIDTH_B), layer), blk(WIDTH_B),
                  blk(WIDTH_C), blk(WIDTH_C), _layer_spec((WIDTH_C, WIDTH_C), layer), _layer_spec((1, WIDTH_C), layer)],
        out_specs=[blk(WIDTH_B), blk(WIDTH_C)],
        out_shape=[jax.ShapeDtypeStruct((t, WIDTH_B), BF16), jax.ShapeDtypeStruct((t, WIDTH_C), BF16)],
        scratch_shapes=[pltpu.VMEM((half + MIRROR_BLOCK, WIDTH_B), BF16), pltpu.VMEM((seq, WIDTH_B), BF16),
                        pltpu.VMEM((seq + 2 * POOL_HALO, WIDTH_C), F32)],
        compiler_params=_params(1),
        name="lat_branches",
    )(ch, sh, cmid, rev, ucs, wf, zb, uc, zc, wp, ps)


def _dft_sym_consts(seq):
    cmat, smat = _dft_consts(seq)
    half = seq // 2
    rev = np.zeros((MIRROR_BLOCK, 2 * MIRROR_BLOCK), np.float32)
    rev[np.arange(MIRROR_BLOCK), MIRROR_BLOCK - np.arange(MIRROR_BLOCK)] = 1.0
    return cmat[:half], smat[:half], np.tile(cmat[half:half + 1], (16, 1)), rev


def _dft_consts(seq):
    n = np.arange(seq, dtype=np.int64)
    ang = 2.0 * np.pi * ((n[:, None] * n[None, :]) % seq).astype(np.float64) / seq
    nrm = 1.0 / np.sqrt(float(seq) * GROUP_B)
    cmat = (np.cos(ang) * nrm).astype(np.float32)
    smat = (np.sin(ang) * nrm).astype(np.float32)
    return cmat, smat


def _group_dft_const():
    m = np.arange(GROUP_B, dtype=np.int64)
    ang = 2.0 * np.pi * ((m[:, None] * m[None, :]) % GROUP_B).astype(np.float64) / GROUP_B
    eye = np.eye(N_GROUPS_B)
    return np.concatenate([np.kron(eye, np.cos(ang)), np.kron(eye, np.sin(ang))], axis=1).astype(np.float32)


def _pool_kernel(u_ref, zc_ref, w_ref, ps_ref, o_ref, pad_ref, *, seq):
    halo = POOL_HALO
    pad_ref[0:halo, :] = jnp.zeros((halo, WIDTH_C), F32)
    pad_ref[halo + seq:2 * halo + seq, :] = jnp.zeros((halo, WIDTH_C), F32)
    pad_ref[halo:halo + seq, :] = u_ref[...]
    ext = POOL_TILE + 2 * halo
    lane = jax.lax.broadcasted_iota(jnp.int32, (POOL_TILE, WIDTH_C), 1)
    g0, g1, g2 = lane < GROUP_C, lane < 2 * GROUP_C, lane < 3 * GROUP_C
    half = jnp.where(g0, POOL_WINDOWS[0] // 2,
                     jnp.where(g1, POOL_WINDOWS[1] // 2,
                               jnp.where(g2, POOL_WINDOWS[2] // 2, POOL_WINDOWS[3] // 2)))
    rowi = jax.lax.broadcasted_iota(jnp.int32, (POOL_TILE, WIDTH_C), 0)
    for ti in range(seq // POOL_TILE):
        t0 = ti * POOL_TILE
        x = pad_ref[t0:t0 + ext, :]
        a1 = x + pltpu.roll(x, 1, 0)
        w4 = pltpu.roll(a1, 1, 0) + pltpu.roll(a1, ext - 1, 0)
        w8 = pltpu.roll(w4, 2, 0) + pltpu.roll(w4, ext - 2, 0)
        w16 = pltpu.roll(w8, 4, 0) + pltpu.roll(w8, ext - 4, 0)
        mid = slice(halo, halo + POOL_TILE)
        wsum = jnp.where(g0, a1[mid], jnp.where(g1, w4[mid], jnp.where(g2, w8[mid], w16[mid])))
        tpos = rowi + t0
        cnt = (jnp.minimum(tpos + half, seq) - jnp.maximum(tpos - half, 0)).astype(F32)
        dlt = wsum / cnt - x[mid]
        y = jnp.dot(dlt.astype(BF16), w_ref[0], preferred_element_type=F32) * ps_ref[0]
        o_ref[t0:t0 + POOL_TILE, :] = (y * zc_ref[t0:t0 + POOL_TILE, :].astype(F32)).astype(BF16)


def _ctx_branches_kernel(q_ref, k_ref, v_ref, za_ref, ucs_ref, zb_ref, uc_ref, zc_ref,
                         c_ref, s_ref, wf_ref, wp_ref, ps_ref, ya_o, yb_o, yc_o, pad_ref, *, seq):
    _ctx_attention(q_ref, k_ref, v_ref, za_ref, ya_o)
    _fourier_kernel(c_ref, s_ref, ucs_ref, wf_ref, zb_ref, yb_o)
    _pool_kernel(uc_ref, zc_ref, wp_ref, ps_ref, yc_o, pad_ref, seq=seq)


def _ctx_branches(q, k, v, za, ucs, zb, uc, zc, cmat, smat, wf, layer, wp, ps, seq):
    t = q.shape[0]
    blk = lambda w: pl.BlockSpec((seq, w), lambda b: (b, 0))
    return pl.pallas_call(
        functools.partial(_ctx_branches_kernel, seq=seq),
        grid=(t // seq,),
        in_specs=[blk(WIDTH_A)] * 4 + [blk(2 * WIDTH_B), blk(WIDTH_B), blk(WIDTH_C), blk(WIDTH_C),
                                       _const_spec((seq, seq)), _const_spec((seq, seq)),
                                       _layer_spec((WIDTH_B, WIDTH_B), layer),
                                       _layer_spec((WIDTH_C, WIDTH_C), layer), _layer_spec((1, WIDTH_C), layer)],
        out_specs=[blk(WIDTH_A), blk(WIDTH_B), blk(WIDTH_C)],
        out_shape=[jax.ShapeDtypeStruct((t, w), BF16) for w in (WIDTH_A, WIDTH_B, WIDTH_C)],
        scratch_shapes=[pltpu.VMEM((seq + 2 * POOL_HALO, WIDTH_C), F32)],
        compiler_params=_params(1),
        name="ctx_branches",
    )(q, k, v, za, ucs, zb, uc, zc, cmat, smat, wf, wp, ps)


def _merge_kernel(x_ref, mod_ref, ng_ref, wg_ref, ya_ref, yb_ref, yc_ref,
                  pa_ref, pb_ref, pc_ref, wo_ref, o_ref):
    x = x_ref[...]
    hb = _modulated_norm(x, mod_ref, ng_ref)
    gate = mod_ref[0, 0, 2:3, :]

    def dots(y_ref, p_ref, col):
        g = jnp.dot(hb, wg_ref[0, :, col * D_MODEL:(col + 1) * D_MODEL], preferred_element_type=F32)
        return g, jnp.dot(y_ref[...], p_ref[0], preferred_element_type=F32)

    def gated(gy):
        return _sigmoid(gy[0]) * gy[1]

    da = dots(ya_ref, pa_ref, 0)
    db = dots(yb_ref, pb_ref, 1)
    m = gated(da)
    dc = dots(yc_ref, pc_ref, 2)
    m = m + gated(db)
    m = m + gated(dc)
    o_ref[...] = x + gate * jnp.dot(m.astype(BF16), wo_ref[0], preferred_element_type=F32)


def _merge(x2, mod, mod_row, mod_seq, layer, ng, w_in, ya, yb, yc, pa, pb, pc, wo, tm=ROW_TILE):
    t = x2.shape[0]
    tiles_per_seq = mod_seq // tm
    row = lambda w: pl.BlockSpec((tm, w), lambda i: (i, 0))
    return pl.pallas_call(
        _merge_kernel,
        grid=(t // tm,),
        in_specs=[
            row(D_MODEL),
            _mod_spec(layer, mod_row, tiles_per_seq),
            _layer_spec((1, D_MODEL), layer),
            pl.BlockSpec((1, D_MODEL, 3 * D_MODEL), lambda i: (layer, 0, 1)),
            row(WIDTH_A), row(WIDTH_B), row(WIDTH_C),
            _layer_spec((WIDTH_A, D_MODEL), layer), _layer_spec((WIDTH_B, D_MODEL), layer),
            _layer_spec((WIDTH_C, D_MODEL), layer), _layer_spec((D_MODEL, D_MODEL), layer),
        ],
        out_specs=row(D_MODEL),
        out_shape=jax.ShapeDtypeStruct((t, D_MODEL), F32),
        compiler_params=_params(1),
        name="merge",
    )(x2, mod, ng, w_in, ya, yb, yc, pa, pb, pc, wo)


def _block_diag(w):
    g, c, e = w.shape
    eye = jnp.eye(g, dtype=w.dtype)
    return (eye[:, None, :, None] * w[:, :, None, :]).reshape(g * c, g * e)


def kernel(x_prompt, x_sample, cache_k, cache_v, c, c_ctx, norm_g, w_ada, b_ada, w_in, q_norm_g, k_norm_g,
           rpb, w_fnet, w_pool, pool_scale, p_a, p_b, p_c, w_o):
    nb_p, seq_p, _ = x_prompt.shape
    nb_s, seq_s, _ = x_sample.shape

    cond = jnp.concatenate([c_ctx[None, :], c, jnp.zeros((8 - 1 - nb_s, D_MODEL), F32)], axis=0)
    ada = _ada(cond, w_ada, b_ada).reshape(DEPTH, 8, 3, D_MODEL)
    bias = _bias_rows(rpb)

    head_mean = jnp.asarray(np.kron(np.eye(N_HEADS), np.full((HEAD_DIM, HEAD_DIM), 1.0 / HEAD_DIM)), BF16)
    bd = jnp.asarray(_group_dft_const()).astype(BF16)
    dft_p = [jnp.asarray(m).astype(BF16) for m in _dft_consts(seq_p)]
    dft_s = [jnp.asarray(m).astype(BF16) for m in _dft_sym_consts(seq_s)]

    w_in_b, wf_b = w_in.astype(BF16), w_fnet.astype(BF16)
    pa, pb, pc, wo = p_a.astype(BF16), p_b.astype(BF16), p_c.astype(BF16), w_o.astype(BF16)

    ng = norm_g[:, None, :]
    qg = jnp.tile(q_norm_g, (1, N_HEADS))[:, None, :]
    kg = jnp.tile(k_norm_g, (1, N_HEADS))[:, None, :]
    wp = jax.vmap(_block_diag)(w_pool).astype(BF16)
    ps = pool_scale[:, None, :]

    cache_kt, cache_vt = jnp.swapaxes(cache_k, -1, -2), jnp.swapaxes(cache_v, -1, -2)

    xp = x_prompt.reshape(nb_p * seq_p, D_MODEL)
    xs = x_sample.reshape(nb_s * seq_s, D_MODEL)
    new_kv = None
    for l in range(DEPTH):
        (q, k, v, za, ucs, zb, uc, zc, kf, vf) = _inproj(
            xp, ada, 0, seq_p, nb_p * seq_p, l, ng, w_in_b, qg, kg, head_mean, bd, prev_kv=new_kv, kv_cache=True)
        new_kv = (kf, vf)
        ya, yb, yc = _ctx_branches(q, k, v, za, ucs, zb, uc, zc, dft_p[0], dft_p[1], wf_b, l, wp, ps, seq_p)
        xp = _merge(xp, ada, 0, nb_p * seq_p, l, ng, w_in_b, ya, yb, yc, pa, pb, pc, wo)

        (q, k, v, za, ucs, zb, uc, zc) = _inproj(
            xs, ada, 1, seq_s, seq_s, l, ng, w_in_b, qg, kg, head_mean, bd)
        ya = _na_attn(q, k, v, cache_kt, cache_vt, l, bias, za, seq_s)
        yb, yc = _lat_branches(ucs, zb, uc, zc, dft_s, wf_b, wp, ps, l, seq_s)
        xs = _merge(xs, ada, 1, seq_s, l, ng, w_in_b, ya, yb, yc, pa, pb, pc, wo)

    return (xp.reshape(nb_p, seq_p, D_MODEL), xs.reshape(nb_s, seq_s, D_MODEL),
            jnp.swapaxes(new_kv[0], -1, -2), jnp.swapaxes(new_kv[1], -1, -2))
```
